```python
import functools
import jax, jax.numpy as jnp
from jax import lax
import numpy as np

D_MODEL = 1024
BATCH = 4
SEQ = 4096
DEPTH = 4
DEC_BATCH = 128
DEC_SEQ = 8
PAST_LEN = 2048
PAGE_SIZE = 128

N_A_LAYERS = DEPTH // 2
N_B_LAYERS = DEPTH - N_A_LAYERS
POOL_WIDTH = D_MODEL
POOL_WINDOWS = (2, 4, 8, 16)
N_POOL_GROUPS = len(POOL_WINDOWS)
POOL_GROUP = POOL_WIDTH // N_POOL_GROUPS
POOL_HIST = max(POOL_WINDOWS) - 1
HEAD_DIM = 128
N_HEADS = D_MODEL // HEAD_DIM
N_KV_HEADS = N_HEADS
ATTN_WIDTH = N_HEADS * HEAD_DIM
KV_WIDTH = N_KV_HEADS * HEAD_DIM
MOBA_BLOCK = 256
MOBA_TOPK = 3
Q_BLOCK = 64
ROPE_THETA = 10000.0
NORM_EPS = 1e-6

kernel_name = 'moba_pool_yoco_decoder_step'


def rmsnorm(x, g):
    xf = x.astype(jnp.float32)
    y = xf * lax.rsqrt(jnp.mean(xf * xf, axis=-1, keepdims=True) + NORM_EPS)
    return (y * g.astype(jnp.float32)).astype(x.dtype)


def ada_mod(c, w, b, n):
    m = jax.nn.silu(c) @ w + b
    return jnp.split(m[:, None, :], n, axis=-1)


def rope(x, pos):
    half = HEAD_DIM // 2
    inv = jnp.power(ROPE_THETA, -jnp.arange(0, HEAD_DIM, 2, dtype=jnp.float32) / HEAD_DIM)
    ang = pos.astype(jnp.float32)[:, None] * inv[None, :]
    cos = jnp.cos(ang)[None, :, None, :]
    sin = jnp.sin(ang)[None, :, None, :]
    xf = x.astype(jnp.float32)
    x1, x2 = xf[..., :half], xf[..., half:]
    return jnp.concatenate([x1 * cos - x2 * sin, x2 * cos + x1 * sin], axis=-1).astype(x.dtype)


def pool_mix(u, hist, pos0, w_grp, pool_scale):
    T = u.shape[1]
    z = jnp.concatenate([hist.astype(u.dtype), u], axis=1).astype(jnp.float32)
    cs = jnp.cumsum(z, axis=1)
    cs = jnp.concatenate([jnp.zeros_like(cs[:, :1]), cs], axis=1)
    pos = pos0 + jnp.arange(T, dtype=jnp.int32)
    end = cs[:, POOL_HIST + 1:]
    outs = []
    for g, w in enumerate(POOL_WINDOWS):
        sl = slice(g * POOL_GROUP, (g + 1) * POOL_GROUP)
        start = cs[:, POOL_HIST + 1 - w:POOL_HIST + 1 - w + T, sl]
        cnt = jnp.minimum(pos + 1, w).astype(jnp.float32)[None, :, None]
        d = (end[..., sl] - start) / cnt - u[..., sl].astype(jnp.float32)
        outs.append(jnp.einsum('btc,cd->btd', d, w_grp[g].astype(jnp.float32)))
    y = jnp.concatenate(outs, axis=-1) * pool_scale.astype(jnp.float32)
    new_hist = z[:, -POOL_HIST:]
    return y.astype(u.dtype), new_hist.astype(u.dtype)


def pool_layer(x, c, hist, pos0, w_ada, b_ada, g_pre, g_post, w_in, w_grp, pool_scale, w_out):
    shift, scale, gate = ada_mod(c, w_ada, b_ada, 3)
    h = rmsnorm(x, g_pre) * (1 + scale) + shift
    ug = h @ w_in
    y, new_hist = pool_mix(ug[..., :POOL_WIDTH], hist, pos0, w_grp, pool_scale)
    o = (y * jax.nn.silu(ug[..., POOL_WIDTH:])) @ w_out
    return x + gate * rmsnorm(o, g_post), new_hist


def shared_kv(x, c, pos, g_kv, w_ada_kv, b_ada_kv, w_kv):
    shift, scale = ada_mod(c, w_ada_kv, b_ada_kv, 2)
    h = rmsnorm(x, g_kv) * (1 + scale) + shift
    kv = h @ w_kv
    B, T = x.shape[:2]
    k = kv[..., :KV_WIDTH].reshape(B, T, N_KV_HEADS, HEAD_DIM)
    v = kv[..., KV_WIDTH:].reshape(B, T, N_KV_HEADS, HEAD_DIM)
    return rope(k, pos), v


def moba_attend(q, q_pos, kb, vb, kmean):
    H, Q = q.shape[:2]
    n_blocks = kb.shape[1]
    n_sel = min(MOBA_TOPK, n_blocks)
    q_blk = q_pos // MOBA_BLOCK
    qf = q.astype(jnp.float32)
    gate_s = jnp.einsum('hqd,hnd->hqn', qf, kmean)
    past = jnp.arange(n_blocks)[None, None, :] < q_blk[None, :, None]
    gate_s = jnp.where(past, gate_s, -jnp.inf)
    _, top_idx = lax.top_k(gate_s, n_sel)
    sel_ok = jnp.arange(n_sel)[None, None, :] < q_blk[None, :, None]
    own = jnp.broadcast_to(q_blk[None, :, None], (H, Q, 1))
    blk_idx = jnp.concatenate([top_idx, own], axis=-1)
    h_idx = jnp.arange(H)[:, None, None]
    k_sel = kb[h_idx, blk_idx]
    v_sel = vb[h_idx, blk_idx]
    s = jnp.einsum('hqd,hqjtd->hqjt', qf, k_sel.astype(jnp.float32)) * (HEAD_DIM ** -0.5)
    k_pos = blk_idx[..., None] * MOBA_BLOCK + jnp.arange(MOBA_BLOCK)[None, None, None, :]
    is_own = (jnp.arange(n_sel + 1) == n_sel)[None, None, :, None]
    ok_sel = jnp.concatenate([sel_ok, jnp.zeros((1, Q, 1), dtype=bool)], axis=-1)[..., None]
    mask = jnp.where(is_own, k_pos <= q_pos[None, :, None, None], ok_sel)
    s = jnp.where(mask, s, -jnp.inf)
    p = jax.nn.softmax(s.reshape(H, Q, -1), axis=-1).reshape(s.shape)
    out = jnp.einsum('hqjt,hqjtd->hqd', p, v_sel.astype(jnp.float32))
    return out.astype(q.dtype)


def moba_prompt(q, k, v):
    B, S = q.shape[:2]
    nb = -(-S // MOBA_BLOCK)
    pad = nb * MOBA_BLOCK - S

    def blocks(t):
        t = jnp.pad(t, ((0, 0), (0, pad), (0, 0), (0, 0)))
        return t.reshape(B, nb, MOBA_BLOCK, N_KV_HEADS, HEAD_DIM).transpose(0, 3, 1, 2, 4)

    kb, vb = blocks(k), blocks(v)
    kmean = jnp.mean(kb.astype(jnp.float32), axis=3)
    n_qb = S // Q_BLOCK
    qc = q.reshape(B, n_qb, Q_BLOCK, N_HEADS, HEAD_DIM).transpose(0, 1, 3, 2, 4)
    pos = jnp.arange(S, dtype=jnp.int32).reshape(n_qb, Q_BLOCK)

    def per_seq(args):
        q_s, kb_s, vb_s, km_s = args

        def per_chunk(a):
            q_c, p_c = a
            return moba_attend(q_c, p_c, kb_s, vb_s, km_s)

        return lax.map(per_chunk, (q_s, pos))

    o = lax.map(per_seq, (qc, kb, vb, kmean))
    return o.transpose(0, 1, 3, 2, 4).reshape(B, S, ATTN_WIDTH)


def moba_sample(q, k_new, v_new, cache_k, cache_v, page_table):
    DB, T = q.shape[:2]
    L = PAST_LEN + T
    nb = -(-L // MOBA_BLOCK)
    pad = nb * MOBA_BLOCK - L
    q_pos = PAST_LEN + jnp.arange(T, dtype=jnp.int32)

    def per_seq(args):
        q_s, kn, vn, pages = args

        def blocks(pool, new):
            past = pool[pages].reshape(PAST_LEN, N_KV_HEADS, HEAD_DIM)
            t = jnp.concatenate([past.astype(new.dtype), new], axis=0)
            t = jnp.pad(t, ((0, pad), (0, 0), (0, 0)))
            return t.reshape(nb, MOBA_BLOCK, N_KV_HEADS, HEAD_DIM).transpose(2, 0, 1, 3)

        kb, vb = blocks(cache_k, kn), blocks(cache_v, vn)
        kmean = jnp.mean(kb.astype(jnp.float32), axis=2)
        return moba_attend(q_s.transpose(1, 0, 2), q_pos, kb, vb, kmean)

    o = lax.map(per_seq, (q, k_new, v_new, page_table))
    return o.transpose(0, 2, 1, 3).reshape(DB, T, ATTN_WIDTH)


def moba_layer(x, c, pos, k, v, attend, w_ada, b_ada, g_pre, g_post, w_in, w_out):
    shift, scale, gate = ada_mod(c, w_ada, b_ada, 3)
    h = rmsnorm(x, g_pre) * (1 + scale) + shift
    qg = h @ w_in
    B, T = x.shape[:2]
    q = rope(qg[..., :ATTN_WIDTH].reshape(B, T, N_HEADS, HEAD_DIM), pos)
    o = attend(q, k, v)
    o = (o * jax.nn.silu(qg[..., ATTN_WIDTH:])) @ w_out
    return x + gate * rmsnorm(o, g_post)


def run_trunk(x, c, pos0, pool_hist, attend, w_ada, b_ada, g_pre, g_post, w_in_pool, w_pool_group,
              pool_scale, w_out_pool, g_kv, w_ada_kv, b_ada_kv, w_kv, w_in_attn, w_out_attn):
    T = x.shape[1]
    pos = pos0 + jnp.arange(T, dtype=jnp.int32)
    new_hist = []
    k = v = None
    for l in range(DEPTH):
        if l < N_A_LAYERS:
            x, hn = pool_layer(x, c, pool_hist[l], pos0, w_ada[l], b_ada[l], g_pre[l], g_post[l],
                               w_in_pool[l], w_pool_group[l], pool_scale[l], w_out_pool[l])
            new_hist.append(hn)
            if l == N_A_LAYERS - 1:
                k, v = shared_kv(x, c, pos, g_kv, w_ada_kv, b_ada_kv, w_kv)
        else:
            j = l - N_A_LAYERS
            x = moba_layer(x, c, pos, k, v, attend, w_ada[l], b_ada[l], g_pre[l], g_post[l],
                           w_in_attn[j], w_out_attn[j])
    return x, k, v, jnp.stack(new_hist)


def setup_inputs(seed: int = 0) -> dict:
    key = jax.random.key(seed)
    ks = jax.random.split(key, 24)
    f32 = jnp.float32
    D = D_MODEL
    n_pages = PAST_LEN // PAGE_SIZE
    n_phys = (5 * DEC_BATCH * n_pages) // 4

    def nrm(k, shape, scale):
        return jax.random.normal(k, shape, f32) * scale

    page_table = jax.random.permutation(ks[0], n_phys)[:DEC_BATCH * n_pages].reshape(DEC_BATCH, n_pages).astype(jnp.int32)
    return {
        'x_prompt': nrm(ks[1], (BATCH, SEQ, D), 1.0),
        'x_sample': nrm(ks[2], (DEC_BATCH, DEC_SEQ, D), 1.0),
        'cache_k': nrm(ks[3], (n_phys, PAGE_SIZE, N_KV_HEADS, HEAD_DIM), 1.0),
        'cache_v': nrm(ks[4], (n_phys, PAGE_SIZE, N_KV_HEADS, HEAD_DIM), 1.0),
        'state_pool': nrm(ks[5], (N_A_LAYERS, DEC_BATCH, POOL_HIST, POOL_WIDTH), 1.0),
        'page_table': page_table,
        'c_prompt': nrm(ks[6], (BATCH, D), 1.0),
        'c_sample': nrm(ks[7], (DEC_BATCH, D), 1.0),
        'w_ada': nrm(ks[8], (DEPTH, D, 3 * D), 0.5 * D ** -0.5),
        'b_ada': nrm(ks[9], (DEPTH, 3 * D), 0.01),
        'g_pre': 1.0 + nrm(ks[10], (DEPTH, D), 0.05),
        'g_post': 1.0 + nrm(ks[11], (DEPTH, D), 0.05),
        'w_in_pool': nrm(ks[12], (N_A_LAYERS, D, 2 * POOL_WIDTH), D ** -0.5),
        'w_pool_group': nrm(ks[13], (N_A_LAYERS, N_POOL_GROUPS, POOL_GROUP, POOL_GROUP), POOL_GROUP ** -0.5),
        'pool_scale': 1.0 + nrm(ks[14], (N_A_LAYERS, POOL_WIDTH), 0.1),
        'w_out_pool': nrm(ks[15], (N_A_LAYERS, POOL_WIDTH, D), POOL_WIDTH ** -0.5),
        'g_kv': 1.0 + nrm(ks[16], (D,), 0.05),
        'w_ada_kv': nrm(ks[17], (D, 2 * D), 0.5 * D ** -0.5),
        'b_ada_kv': nrm(ks[18], (2 * D,), 0.01),
        'w_kv': nrm(ks[19], (D, 2 * KV_WIDTH), D ** -0.5),
        'w_in_attn': nrm(ks[20], (N_B_LAYERS, D, 2 * ATTN_WIDTH), D ** -0.5),
        'w_out_attn': nrm(ks[21], (N_B_LAYERS, ATTN_WIDTH, D), ATTN_WIDTH ** -0.5),
    }


def reference(x_prompt, x_sample, cache_k, cache_v, state_pool, page_table, c_prompt, c_sample,
              w_ada, b_ada, g_pre, g_post, w_in_pool, w_pool_group, pool_scale, w_out_pool,
              g_kv, w_ada_kv, b_ada_kv, w_kv, w_in_attn, w_out_attn):
    hist0 = jnp.zeros((N_A_LAYERS, x_prompt.shape[0], POOL_HIST, POOL_WIDTH), x_prompt.dtype)
    y_prompt, k_prompt, v_prompt, pool_prompt = run_trunk(
        x_prompt, c_prompt, 0, hist0, moba_prompt,
        w_ada, b_ada, g_pre, g_post, w_in_pool, w_pool_group, pool_scale, w_out_pool,
        g_kv, w_ada_kv, b_ada_kv, w_kv, w_in_attn, w_out_attn)
    attend_sample = functools.partial(moba_sample, cache_k=cache_k, cache_v=cache_v, page_table=page_table)
    y_sample, k_sample, v_sample, pool_sample = run_trunk(
        x_sample, c_sample, PAST_LEN, state_pool, attend_sample,
        w_ada, b_ada, g_pre, g_post, w_in_pool, w_pool_group, pool_scale, w_out_pool,
        g_kv, w_ada_kv, b_ada_kv, w_kv, w_in_attn, w_out_attn)
    return (y_prompt, y_sample, k_prompt, v_prompt, k_sample, v_sample, pool_prompt, pool_sample)
```

```python
import functools
import math

import jax
import jax.numpy as jnp
from jax import lax
from jax.experimental import pallas as pl
from jax.experimental.pallas import tpu as pltpu

F32 = jnp.float32
BF16 = jnp.bfloat16
NEG_INF = float("-inf")

POOL_WINDOWS = (2, 4, 8, 16)
MOBA_BLOCK = 256
MOBA_TOPK = 3
ROPE_THETA = 10000.0
NORM_EPS = 1e-6

V7X_LANES = 128
V7X_SUBLANES = 8
V7X_VMEM_LIMIT_BYTES = 56 * 1024 * 1024

HIST_PAD = 16


def _params(n_axes):
    return pltpu.CompilerParams(
        dimension_semantics=("arbitrary",) * n_axes, vmem_limit_bytes=V7X_VMEM_LIMIT_BYTES)


def _silu(x):
    return x * (1.0 / (1.0 + jnp.exp(-x)))


def _rms(x, g):
    return x * lax.rsqrt(jnp.mean(x * x, axis=-1, keepdims=True) + NORM_EPS) * g


def _dot(a, b):
    return jnp.dot(a, b, preferred_element_type=F32)


def _bucket(idx, size, n):
    out = jnp.zeros_like(idx)
    for h in range(1, n):
        out = out + jnp.where(idx >= h * size, 1, 0)
    return out


def _dot_nt(a, b):
    return lax.dot_general(a, b, (((1,), (1,)), ((), ())), preferred_element_type=F32)


def _rope_table_kernel(cos_ref, sin_ref, *, pos0):
    t, hd = cos_ref.shape
    half = hd // 2
    lane = lax.broadcasted_iota(jnp.int32, (t, hd), 1)
    row = lax.broadcasted_iota(jnp.int32, (t, hd), 0)
    j = jnp.where(lane < half, lane, lane - half).astype(F32)
    inv = jnp.exp(j * (-2.0 * math.log(ROPE_THETA) / hd))
    ang = (row + pos0).astype(F32) * inv
    cos_ref[...] = jnp.cos(ang)
    s = jnp.sin(ang)
    sin_ref[...] = jnp.where(lane < half, -s, s)


def _rope_tables(t, hd, pos0):
    return pl.pallas_call(
        functools.partial(_rope_table_kernel, pos0=pos0),
        out_shape=(jax.ShapeDtypeStruct((t, hd), F32), jax.ShapeDtypeStruct((t, hd), F32)),
        name="rope_tables",
    )()


def _rope_head(a, h, cos, sin, bb, tt, hd):
    slab = a[:, h * hd:(h + 1) * hd]
    rot = pltpu.roll(slab, hd // 2, 1)
    return slab.reshape(bb, tt, hd) * cos[None] + rot.reshape(bb, tt, hd) * sin[None]


def _ada_kernel(c_ref, w_ref, b_ref, o_ref):
    h = _silu(c_ref[...]).astype(BF16)
    o_ref[...] = _dot(h, w_ref[...].astype(BF16)) + b_ref[...]


def _ada_mods(c, w, b):
    n_l, d, n = w.shape
    r = c.shape[0]
    bn = 1024
    return pl.pallas_call(
        _ada_kernel,
        grid=(n_l, n // bn),
        in_specs=[
            pl.BlockSpec((r, d), lambda l, j: (0, 0)),
            pl.BlockSpec((None, d, bn), lambda l, j: (l, 0, j)),
            pl.BlockSpec((None, 1, bn), lambda l, j: (l, 0, j)),
        ],
        out_specs=pl.BlockSpec((None, r, bn), lambda l, j: (l, 0, j)),
        out_shape=jax.ShapeDtypeStruct((n_l, r, n), F32),
        compiler_params=_params(2),
        name="ada_mods",
    )(c, w, b.reshape(n_l, 1, n))


def _pool_kernel(x_ref, mod_ref, hist_ref, gpre_ref, gpost_ref, win_ref, wgrp_ref, pscale_ref, wout_ref,
                 y_ref, hist_out_ref, z_scr, *, pos0):
    t = pl.program_id(1)
    n_t = pl.num_programs(1)
    bb, tt, d = x_ref.shape
    w = z_scr.shape[-1]
    n_hist = hist_ref.shape[1]
    m = bb * tt
    grp = w // len(POOL_WINDOWS)

    @pl.when(t == 0)
    def _():
        z_scr[:, HIST_PAD - n_hist:HIST_PAD, :] = hist_ref[...]

    @pl.when(t > 0)
    def _():
        z_scr[:, 0:HIST_PAD, :] = z_scr[:, tt:tt + HIST_PAD, :]

    mod = mod_ref[...]
    shift, scale, gate = mod[:, :, 0:d], mod[:, :, d:2 * d], mod[:, :, 2 * d:3 * d]
    x = x_ref[...]
    h = _rms(x, gpre_ref[...]) * (1.0 + scale) + shift
    ug = _dot(h.reshape(m, d).astype(BF16), win_ref[...])
    u = ug[:, 0:w]
    z_scr[:, HIST_PAD:HIST_PAD + tt, :] = u.reshape(bb, tt, w)

    pos = pos0 + t * tt + lax.broadcasted_iota(jnp.int32, (1, tt, 1), 1)
    ys = []
    for g, win in enumerate(POOL_WINDOWS):
        sl = slice(g * grp, (g + 1) * grp)
        acc = z_scr[:, HIST_PAD:HIST_PAD + tt, sl]
        for k in range(1, win):
            acc = acc + z_scr[:, HIST_PAD - k:HIST_PAD - k + tt, sl]
        cnt = jnp.minimum(pos + 1, win).astype(F32)
        dm = acc / cnt - z_scr[:, HIST_PAD:HIST_PAD + tt, sl]
        ys.append(_dot(dm.reshape(m, grp).astype(BF16), wgrp_ref[g]))
    y = jnp.concatenate(ys, axis=-1) * pscale_ref[...]
    a = (y * _silu(ug[:, w:2 * w])).astype(BF16)
    o = _dot(a, wout_ref[...]).reshape(bb, tt, d)
    y_ref[...] = x + gate * _rms(o, gpost_ref[...])

    @pl.when(t == n_t - 1)
    def _():
        hist_out_ref[...] = z_scr[:, HIST_PAD + tt - n_hist:HIST_PAD + tt, :]


def _pool_layer(x, mod, hist, g_pre, g_post, w_in, w_grp, pool_scale, w_out, pos0, bb, tt):
    b, t, d = x.shape
    w = w_out.shape[0]
    n_hist = hist.shape[1]
    n_g, grp, _ = w_grp.shape
    const2 = lambda i, j: (0, 0)
    return pl.pallas_call(
        functools.partial(_pool_kernel, pos0=pos0),
        grid=(b // bb, t // tt),
        in_specs=[
            pl.BlockSpec((bb, tt, d), lambda i, j: (i, j, 0)),
            pl.BlockSpec((bb, 1, 3 * d), lambda i, j: (i, 0, 0)),
            pl.BlockSpec((bb, n_hist, w), lambda i, j: (i, 0, 0)),
            pl.BlockSpec((1, d), const2),
            pl.BlockSpec((1, d), const2),
            pl.BlockSpec((d, 2 * w), const2),
            pl.BlockSpec((n_g, grp, grp), lambda i, j: (0, 0, 0)),
            pl.BlockSpec((1, w), const2),
            pl.BlockSpec((w, d), const2),
        ],
        out_specs=(
            pl.BlockSpec((bb, tt, d), lambda i, j: (i, j, 0)),
            pl.BlockSpec((bb, n_hist, w), lambda i, j: (i, 0, 0)),
        ),
        out_shape=(jax.ShapeDtypeStruct((b, t, d), F32), jax.ShapeDtypeStruct((b, n_hist, w), F32)),
        scratch_shapes=[pltpu.VMEM((bb, HIST_PAD + tt, w), F32)],
        compiler_params=_params(2),
        name="pool_layer",
    )(x, mod, hist, g_pre.reshape(1, d), g_post.reshape(1, d), w_in, w_grp, pool_scale.reshape(1, w), w_out)


def _normed(x_ref, mod_ref, g_ref):
    bb, tt, d = x_ref.shape
    mod = mod_ref[...]
    shift, scale = mod[:, :, 0:d], mod[:, :, d:2 * d]
    h = _rms(x_ref[...], g_ref[...]) * (1.0 + scale) + shift
    return h.reshape(bb * tt, d).astype(BF16)


def _kv_kernel(x_ref, mod_ref, g_ref, w_ref, cos_ref, sin_ref, k_ref, v_ref, *extra, n_heads, hd):
    bb, tt, _ = x_ref.shape
    kvw = n_heads * hd
    kv = _dot(_normed(x_ref, mod_ref, g_ref), w_ref[...])
    cos, sin = cos_ref[...], sin_ref[...]
    n_blk = tt // MOBA_BLOCK
    for h in range(n_heads):
        k_h = _rope_head(kv, h, cos, sin, bb, tt, hd)
        v_h = kv[:, kvw + h * hd:kvw + (h + 1) * hd]
        k_ref[:, :, h, :] = k_h
        v_ref[:, :, h, :] = v_h.reshape(bb, tt, hd)
        if extra:
            kb_ref, vt_ref, kmean_ref = extra
            kb_ref[:, :, h * hd:(h + 1) * hd] = k_h.astype(BF16)
            kmean_ref[:, h * hd:(h + 1) * hd] = jnp.mean(k_h.reshape(n_blk, MOBA_BLOCK, hd), axis=1)
            for j in range(n_blk):
                vt_ref[h, j] = v_h[j * MOBA_BLOCK:(j + 1) * MOBA_BLOCK, :].T.astype(BF16)


def _kv_proj(x, mod, g, w, cos, sin, n_heads, hd, bb, tt, with_attn_layouts):
    b, t, d = x.shape
    kvw = n_heads * hd
    const2 = lambda i, j: (0, 0)
    tile = lambda i, j: (i, j, 0)
    tile4 = lambda i, j: (i, j, 0, 0)
    out_specs = [pl.BlockSpec((bb, tt, n_heads, hd), tile4), pl.BlockSpec((bb, tt, n_heads, hd), tile4)]
    out_shape = [jax.ShapeDtypeStruct((b, t, n_heads, hd), F32), jax.ShapeDtypeStruct((b, t, n_heads, hd), F32)]
    if with_attn_layouts:
        assert bb == 1 and tt % MOBA_BLOCK == 0
        n_blk = tt // MOBA_BLOCK
        out_specs += [
            pl.BlockSpec((bb, tt, kvw), tile),
            pl.BlockSpec((None, n_heads, n_blk, hd, MOBA_BLOCK), lambda i, j: (i, 0, j, 0, 0)),
            pl.BlockSpec((None, None, n_blk, kvw), lambda i, j: (i, j, 0, 0)),
        ]
        out_shape += [
            jax.ShapeDtypeStruct((b, t, kvw), BF16),
            jax.ShapeDtypeStruct((b, n_heads, t // MOBA_BLOCK, hd, MOBA_BLOCK), BF16),
            jax.ShapeDtypeStruct((b, t // tt, n_blk, kvw), F32),
        ]
    return pl.pallas_call(
        functools.partial(_kv_kernel, n_heads=n_heads, hd=hd),
        grid=(b // bb, t // tt),
        in_specs=[
            pl.BlockSpec((bb, tt, d), tile),
            pl.BlockSpec((bb, 1, mod.shape[-1]), lambda i, j: (i, 0, 0)),
            pl.BlockSpec((1, d), const2),
            pl.BlockSpec((d, 2 * kvw), const2),
            pl.BlockSpec((tt, hd), lambda i, j: (j, 0)),
            pl.BlockSpec((tt, hd), lambda i, j: (j, 0)),
        ],
        out_specs=tuple(out_specs),
        out_shape=tuple(out_shape),
        compiler_params=_params(2),
        name="kv_proj",
    )(x, mod, g.reshape(1, d), w, cos, sin)


def _q_kernel(x_ref, mod_ref, g_ref, w_ref, cos_ref, sin_ref, q_ref, gate_ref, *, n_heads, hd, head_split):
    bb, tt, _ = x_ref.shape
    aw = n_heads * hd
    qg = _dot(_normed(x_ref, mod_ref, g_ref), w_ref[...])
    cos, sin = cos_ref[...], sin_ref[...]
    for h in range(n_heads):
        q_h = _rope_head(qg, h, cos, sin, bb, tt, hd)
        if head_split:
            q_ref[:, :, h, :] = q_h
        else:
            q_ref[:, :, h * hd:(h + 1) * hd] = q_h.astype(BF16)
    gate_ref[...] = qg[:, aw:2 * aw].reshape(bb, tt, aw)


def _q_proj(x, mod, g, w, cos, sin, n_heads, hd, bb, tt, head_split):
    b, t, d = x.shape
    aw = n_heads * hd
    const2 = lambda i, j: (0, 0)
    tile = lambda i, j: (i, j, 0)
    if head_split:
        q_spec = pl.BlockSpec((bb, tt, n_heads, hd), lambda i, j: (i, j, 0, 0))
        q_shape = jax.ShapeDtypeStruct((b, t, n_heads, hd), F32)
    else:
        q_spec = pl.BlockSpec((bb, tt, aw), tile)
        q_shape = jax.ShapeDtypeStruct((b, t, aw), BF16)
    return pl.pallas_call(
        functools.partial(_q_kernel, n_heads=n_heads, hd=hd, head_split=head_split),
        grid=(b // bb, t // tt),
        in_specs=[
            pl.BlockSpec((bb, tt, d), tile),
            pl.BlockSpec((bb, 1, mod.shape[-1]), lambda i, j: (i, 0, 0)),
            pl.BlockSpec((1, d), const2),
            pl.BlockSpec((d, 2 * aw), const2),
            pl.BlockSpec((tt, hd), lambda i, j: (j, 0)),
            pl.BlockSpec((tt, hd), lambda i, j: (j, 0)),
        ],
        out_specs=(q_spec, pl.BlockSpec((bb, tt, aw), tile)),
        out_shape=(q_shape, jax.ShapeDtypeStruct((b, t, aw), F32)),
        compiler_params=_params(2),
        name="q_proj",
    )(x, mod, g.reshape(1, d), w, cos, sin)


def _out_kernel(o_ref, gate_ref, x_ref, mod_ref, gpost_ref, w_ref, y_ref, *, head_split):
    bb, tt, d = x_ref.shape
    aw = gate_ref.shape[-1]
    sg = _silu(gate_ref[...])
    if head_split:
        n_heads, hd = o_ref.shape[2], o_ref.shape[3]
        r = jnp.zeros((bb * tt, d), F32)
        for h in range(n_heads):
            a_h = (o_ref[:, :, h, :] * sg[:, :, h * hd:(h + 1) * hd]).reshape(bb * tt, hd).astype(BF16)
            r = r + _dot(a_h, w_ref[h * hd:(h + 1) * hd, :])
    else:
        a = (o_ref[...] * sg).reshape(bb * tt, aw).astype(BF16)
        r = _dot(a, w_ref[...])
    res_gate = mod_ref[...][:, :, 2 * d:3 * d]
    y_ref[...] = x_ref[...] + res_gate * _rms(r.reshape(bb, tt, d), gpost_ref[...])


def _out_proj(o, gate, x, mod, g_post, w, bb, tt, head_split):
    b, t, d = x.shape
    aw = gate.shape[-1]
    tile = lambda i, j: (i, j, 0)
    if head_split:
        o_spec = pl.BlockSpec((bb, tt) + o.shape[2:], lambda i, j: (i, j, 0, 0))
    else:
        o_spec = pl.BlockSpec((bb, tt, aw), tile)
    return pl.pallas_call(
        functools.partial(_out_kernel, head_split=head_split),
        grid=(b // bb, t // tt),
        in_specs=[
            o_spec,
            pl.BlockSpec((bb, tt, aw), tile),
            pl.BlockSpec((bb, tt, d), tile),
            pl.BlockSpec((bb, 1, mod.shape[-1]), lambda i, j: (i, 0, 0)),
            pl.BlockSpec((1, d), lambda i, j: (0, 0)),
            pl.BlockSpec((aw, d), lambda i, j: (0, 0)),
        ],
        out_specs=pl.BlockSpec((bb, tt, d), tile),
        out_shape=jax.ShapeDtypeStruct((b, t, d), F32),
        compiler_params=_params(2),
        name="out_proj",
    )(o, gate, x, mod, g_post.reshape(1, d), w)


def _select_blocks_t(gate_t, n_past):
    nb = gate_t.shape[0]
    blk = lax.broadcasted_iota(jnp.int32, gate_t.shape, 0)
    rank = jnp.zeros(gate_t.shape, jnp.int32)
    for m in range(nb):
        gm = gate_t[m:m + 1, :]
        ahead = jnp.where(gm > gate_t, 1, jnp.where(gm == gate_t, jnp.where(m < blk, 1, 0), 0))
        rank = rank + jnp.where(m < n_past, ahead, 0)
    return jnp.where(blk < n_past, jnp.where(rank < MOBA_TOPK, 1, 0), 0)


def _moba_prompt_kernel(q_ref, k_ref, vt_ref, kmean_ref, o_ref, bias_scr, *, scale):
    i = pl.program_id(2)
    q_t = q_ref[...].astype(F32).T.astype(BF16)

    gate_t = _dot(kmean_ref[...].astype(BF16), q_t)
    keep = _select_blocks_t(gate_t, i)
    bias_scr[...] = jnp.where(keep > 0, 0.0, NEG_INF)

    start = pl.multiple_of(i * MOBA_BLOCK, MOBA_BLOCK)
    s = _dot(k_ref[pl.ds(start, MOBA_BLOCK), :], q_t) * scale
    key_i = lax.broadcasted_iota(jnp.int32, s.shape, 0)
    qry_i = lax.broadcasted_iota(jnp.int32, s.shape, 1)
    s = jnp.where(key_i <= qry_i, s, NEG_INF)
    m0 = jnp.max(s, axis=0, keepdims=True)
    p = jnp.exp(s - m0)
    l0 = jnp.sum(p, axis=0, keepdims=True)
    acc0 = _dot(vt_ref[i], p.astype(BF16))

    def body(n, carry):
        m_run, l_run, acc = carry
        st = pl.multiple_of(n * MOBA_BLOCK, MOBA_BLOCK)
        sn = _dot(k_ref[pl.ds(st, MOBA_BLOCK), :], q_t) * scale + bias_scr[pl.ds(n, 1), :]
        m_new = jnp.maximum(m_run, jnp.max(sn, axis=0, keepdims=True))
        alpha = jnp.exp(m_run - m_new)
        pn = jnp.exp(sn - m_new)
        l_new = alpha * l_run + jnp.sum(pn, axis=0, keepdims=True)
        acc_new = alpha * acc + _dot(vt_ref[n], pn.astype(BF16))
        return m_new, l_new, acc_new

    _, l_fin, acc = lax.fori_loop(0, i, body, (m0, l0, acc0))
    o_ref[...] = (acc / l_fin).T


def _moba_prompt(q, kb, vt, kmean, n_heads, hd):
    b, s, aw = q.shape
    n_blk = s // MOBA_BLOCK
    return pl.pallas_call(
        functools.partial(_moba_prompt_kernel, scale=hd ** -0.5),
        grid=(b, n_heads, n_blk),
        in_specs=[
            pl.BlockSpec((None, MOBA_BLOCK, hd), lambda bi, h, i: (bi, i, h)),
            pl.BlockSpec((None, s, hd), lambda bi, h, i: (bi, 0, h)),
            pl.BlockSpec((None, None, n_blk, hd, MOBA_BLOCK), lambda bi, h, i: (bi, h, 0, 0, 0)),
            pl.BlockSpec((None, n_blk, hd), lambda bi, h, i: (bi, 0, h)),
        ],
        out_specs=pl.BlockSpec((None, MOBA_BLOCK, hd), lambda bi, h, i: (bi, i, h)),
        out_shape=jax.ShapeDtypeStruct((b, s, aw), F32),
        scratch_shapes=[pltpu.VMEM((n_blk, MOBA_BLOCK), F32)],
        compiler_params=_params(3),
        name="moba_prompt",
    )(q, kb, vt, kmean)


def _moba_sample_kernel(pt_ref, q_ref, kn_ref, vn_ref, expand_ref, *refs, scale, n_pages, pps):
    del pt_ref
    k_refs, v_refs, o_ref = refs[0:pps], refs[pps:2 * pps], refs[2 * pps]
    kmean_scr, keepx_scr, s_scr, p_scr, pown_scr, l_scr, acc_scr = refs[2 * pps + 1:]
    j = pl.program_id(1)
    ks = n_pages // pps
    t_new, n_heads, hd = q_ref.shape
    page = k_refs[0].shape[0]
    rows = t_new * n_heads
    pr = page * n_heads
    ppb = MOBA_BLOCK // page
    n_blocks = n_pages // ppb
    lanes = kmean_scr.shape[0]
    q_all = q_ref[...].reshape(rows, hd).astype(BF16)

    @pl.when(j == 0)
    def _():
        kmean_scr[...] = jnp.zeros_like(kmean_scr)

    @pl.when(j < ks)
    def _():
        sums = []
        for a in range(pps):
            kp = k_refs[a][...]
            sums.append(jnp.sum(kp, axis=0))
            s_scr[j * pps + a] = _dot_nt(q_all, kp.reshape(pr, hd).astype(BF16)) * scale
        for b in range(pps // ppb):
            blk_sum = sums[b * ppb]
            for e in range(1, ppb):
                blk_sum = blk_sum + sums[b * ppb + e]
            r0 = pl.multiple_of((j * (pps // ppb) + b) * n_heads, n_heads)
            kmean_scr[pl.ds(r0, n_heads), :] = blk_sum * (1.0 / MOBA_BLOCK)

    @pl.when(j == ks - 1)
    def _():
        r_i = lax.broadcasted_iota(jnp.int32, (rows, 1), 0)
        c_i = lax.broadcasted_iota(jnp.int32, (1, lanes), 1)
        r_tok = _bucket(r_i, n_heads, t_new)
        c_grp = _bucket(c_i, n_heads, lanes // n_heads)
        same_head = (r_i - n_heads * r_tok) == (c_i - n_heads * c_grp)

        gate = _dot_nt(q_all, kmean_scr[...].astype(BF16))
        valid = jnp.where(c_i < n_blocks * n_heads, jnp.where(same_head, 1, 0), 0)
        g = jnp.where(valid > 0, gate, NEG_INF)
        rank = jnp.zeros((rows, lanes), jnp.int32)
        for kk in range(1, lanes // n_heads):
            gr = pltpu.roll(g, kk * n_heads, 1)
            tie = jnp.where(c_i >= kk * n_heads, 1, 0)
            rank = rank + jnp.where(gr > g, 1, jnp.where(gr == g, tie, 0))
        keep = jnp.where(valid > 0, jnp.where(rank < MOBA_TOPK, 1.0, 0.0), 0.0)
        keep_x = _dot(keep.astype(BF16), expand_ref[...])
        for n in range(n_blocks):
            keepx_scr[n] = keep_x[:, n * lanes:(n + 1) * lanes]

        pad = jnp.zeros((lanes - rows, hd), F32)
        kn = jnp.concatenate([kn_ref[...].reshape(rows, hd), pad], axis=0).astype(BF16)
        s_own = _dot_nt(q_all, kn) * scale
        own_ok = jnp.where(c_i < rows, jnp.where(same_head, jnp.where(c_grp <= r_tok, 1, 0), 0), 0)
        s_own = jnp.where(own_ok > 0, s_own, NEG_INF)
        m0 = jnp.max(s_own, axis=-1, keepdims=True)

        def keep_tile(n):
            kt = keepx_scr[n]
            return jnp.concatenate([kt] * (pr // lanes), axis=1) > 0.5

        def max_body(n, m_run):
            ktile = keep_tile(n)
            for e in range(ppb):
                sn = jnp.where(ktile, s_scr[n * ppb + e], NEG_INF)
                m_run = jnp.maximum(m_run, jnp.max(sn, axis=-1, keepdims=True))
            return m_run

        m_fin = lax.fori_loop(0, n_blocks, max_body, m0)
        p_own = jnp.exp(s_own - m_fin)

        def p_body(n, l_run):
            ktile = keep_tile(n)
            for e in range(ppb):
                pn = jnp.exp(jnp.where(ktile, s_scr[n * ppb + e], NEG_INF) - m_fin)
                l_run = l_run + jnp.sum(pn, axis=-1, keepdims=True)
                p_scr[n * ppb + e] = pn.astype(BF16)
            return l_run

        l_fin = lax.fori_loop(0, n_blocks, p_body, jnp.sum(p_own, axis=-1, keepdims=True))
        l_scr[...] = jnp.broadcast_to(l_fin, l_scr.shape)
        vn = jnp.concatenate([vn_ref[...].reshape(rows, hd), pad], axis=0).astype(BF16)
        acc_scr[...] = _dot(p_own.astype(BF16), vn)

    @pl.when(j >= ks)
    def _():
        acc = acc_scr[...]
        for a in range(pps):
            vp = v_refs[a][...].reshape(pr, hd).astype(BF16)
            acc = acc + _dot(p_scr[(j - ks) * pps + a], vp)
        acc_scr[...] = acc

    @pl.when(j == 2 * ks - 1)
    def _():
        o_ref[...] = (acc_scr[...] / l_scr[...]).reshape(t_new, n_heads, hd)


def _moba_sample(q, k_new, v_new, cache_k, cache_v, page_table, expand):
    db, t_new, n_heads, hd = q.shape
    page = cache_k.shape[1]
    n_pages = page_table.shape[1]
    pps = 4
    assert MOBA_BLOCK % page == 0 and pps % (MOBA_BLOCK // page) == 0 and n_pages % pps == 0
    ks = n_pages // pps
    rows = t_new * n_heads
    lanes = V7X_LANES
    assert rows <= lanes and (n_pages * page // MOBA_BLOCK) * n_heads <= lanes and lanes % n_heads == 0

    def k_page(a):
        return lambda s, j, pt: (pt[s, jnp.minimum(j, ks - 1) * pps + a], 0, 0, 0)

    def v_page(a):
        return lambda s, j, pt: (pt[s, jnp.maximum(j - ks, 0) * pps + a], 0, 0, 0)

    seq = lambda s, j, pt: (s, 0, 0, 0)
    new_spec = pl.BlockSpec((None, t_new, n_heads, hd), seq)
    page_block = (None, page, n_heads, hd)
    grid_spec = pltpu.PrefetchScalarGridSpec(
        num_scalar_prefetch=1,
        grid=(db, 2 * ks),
        in_specs=[new_spec, new_spec, new_spec, pl.BlockSpec(expand.shape, lambda s, j, pt: (0, 0))]
        + [pl.BlockSpec(page_block, k_page(a)) for a in range(pps)]
        + [pl.BlockSpec(page_block, v_page(a)) for a in range(pps)],
        out_specs=new_spec,
        scratch_shapes=[
            pltpu.VMEM((lanes, hd), F32),
            pltpu.VMEM((n_pages * page // MOBA_BLOCK, rows, lanes), F32),
            pltpu.VMEM((n_pages, rows, page * n_heads), F32),
            pltpu.VMEM((n_pages, rows, page * n_heads), BF16),
            pltpu.VMEM((rows, lanes), BF16),
            pltpu.VMEM((rows, hd), F32),
            pltpu.VMEM((rows, hd), F32),
        ],
    )
    return pl.pallas_call(
        functools.partial(_moba_sample_kernel, scale=hd ** -0.5, n_pages=n_pages, pps=pps),
        grid_spec=grid_spec,
        out_shape=jax.ShapeDtypeStruct((db, t_new, n_heads, hd), F32),
        compiler_params=_params(2),
        name="moba_sample",
    )(page_table, q, k_new, v_new, expand, *([cache_k] * pps), *([cache_v] * pps))


def _expand_matrix(n_blocks, n_heads, lanes):
    r = jnp.arange(lanes)[:, None]
    c = jnp.arange(n_blocks * lanes)[None, :]
    return ((r // n_heads == c // lanes) & (r % n_heads == c % n_heads) & (r < n_blocks * n_heads)).astype(BF16)


def _trunk(x, mods, mods_kv, hist, pos0, wts, n_heads, hd, bb, tt, attend, is_prompt):
    t = x.shape[1]
    n_a = wts["w_in_pool"].shape[0]
    n_b = wts["w_in_attn"].shape[0]
    cos, sin = _rope_tables(t, hd, pos0)
    new_hist = []
    for l in range(n_a):
        x, hn = _pool_layer(x, mods[l], hist[l], wts["g_pre"][l], wts["g_post"][l], wts["w_in_pool"][l],
                            wts["w_pool_group"][l], wts["pool_scale"][l], wts["w_out_pool"][l], pos0, bb, tt)
        new_hist.append(hn)
    kv = _kv_proj(x, mods_kv, wts["g_kv"], wts["w_kv"], cos, sin, n_heads, hd, bb, tt, is_prompt)
    for jl in range(n_b):
        l = n_a + jl
        q, gate = _q_proj(x, mods[l], wts["g_pre"][l], wts["w_in_attn"][jl], cos, sin, n_heads, hd, bb, tt,
                          head_split=not is_prompt)
        o = attend(q, kv)
        x = _out_proj(o, gate, x, mods[l], wts["g_post"][l], wts["w_out_attn"][jl], bb, tt,
                      head_split=not is_prompt)
    return x, kv[0], kv[1], jnp.stack(new_hist)


def kernel(x_prompt, x_sample, cache_k, cache_v, state_pool, page_table, c_prompt, c_sample, w_ada, b_ada,
           g_pre, g_post, w_in_pool, w_pool_group, pool_scale, w_out_pool, g_kv, w_ada_kv, b_ada_kv, w_kv,
           w_in_attn, w_out_attn):
    n_b_p = x_prompt.shape[0]
    page, n_heads, hd = cache_k.shape[1], cache_k.shape[2], cache_k.shape[3]
    past_len = page_table.shape[1] * page
    n_a = w_in_pool.shape[0]
    n_hist = state_pool.shape[2]

    c_all = jnp.concatenate([c_prompt, c_sample], axis=0)
    r = c_all.shape[0]
    r_pad = -(-r // V7X_SUBLANES) * V7X_SUBLANES
    c_all = jnp.pad(c_all, ((0, r_pad - r), (0, 0)))
    mods_all = _ada_mods(c_all, w_ada, b_ada)
    mods_kv_all = _ada_mods(c_all, w_ada_kv[None], b_ada_kv[None])[0]
    mods_p = [mods_all[l, 0:n_b_p][:, None, :] for l in range(mods_all.shape[0])]
    mods_s = [mods_all[l, n_b_p:r][:, None, :] for l in range(mods_all.shape[0])]
    mods_kv_p = mods_kv_all[0:n_b_p][:, None, :]
    mods_kv_s = mods_kv_all[n_b_p:r][:, None, :]

    wts = dict(
        g_pre=g_pre, g_post=g_post, pool_scale=pool_scale, g_kv=g_kv,
        w_in_pool=w_in_pool.astype(BF16), w_pool_group=w_pool_group.astype(BF16),
        w_out_pool=w_out_pool.astype(BF16), w_kv=w_kv.astype(BF16),
        w_in_attn=w_in_attn.astype(BF16), w_out_attn=w_out_attn.astype(BF16))

    def attend_prompt(q, kv):
        _, _, kb, vt, kmean = kv
        kmean = kmean.reshape(kmean.shape[0], -1, kmean.shape[-1])
        return _moba_prompt(q, kb, vt, kmean, n_heads, hd)

    expand = _expand_matrix(past_len // MOBA_BLOCK, n_heads, V7X_LANES)

    def attend_sample(q, kv):
        return _moba_sample(q, kv[0], kv[1], cache_k, cache_v, page_table, expand)

    hist0 = jnp.zeros((n_a, n_b_p, n_hist, state_pool.shape[3]), F32)
    y_p, k_p, v_p, pool_p = _trunk(x_prompt, mods_p, mods_kv_p, hist0, 0, wts, n_heads, hd,
                                   bb=1, tt=512, attend=attend_prompt, is_prompt=True)
    y_s, k_s, v_s, pool_s = _trunk(x_sample, mods_s, mods_kv_s, state_pool, past_len, wts, n_heads, hd,
                                   bb=16, tt=x_sample.shape[1], attend=attend_sample, is_prompt=False)
    return (y_p, y_s, k_p, v_p, k_s, v_s, pool_p, pool_s)
```

```python
import functools
import math

import jax
import jax.numpy as jnp
from jax import lax
from jax.experimental import pallas as pl
from jax.experimental.pallas import tpu as pltpu

F32 = jnp.float32
BF16 = jnp.bfloat16
NEG_INF = float("-inf")
LOG2E = math.log2(math.e)

POOL_WINDOWS = (2, 4, 8, 16)
MOBA_BLOCK = 256
MOBA_TOPK = 3
ROPE_THETA = 10000.0
NORM_EPS = 1e-6

V7X_LANES = 128
V7X_SUBLANES = 8
V7X_VMEM_LIMIT_BYTES = 56 * 1024 * 1024

HIST_PAD = 16


def _params(n_axes):
    return pltpu.CompilerParams(
        dimension_semantics=("arbitrary",) * n_axes, vmem_limit_bytes=V7X_VMEM_LIMIT_BYTES)


def _silu(x):
    return x * (1.0 / (1.0 + jnp.exp(-x)))


def _rms(x, g):
    return x * lax.rsqrt(jnp.mean(x * x, axis=-1, keepdims=True) + NORM_EPS) * g


def _dot(a, b):
    return jnp.dot(a, b, preferred_element_type=F32)


def _bucket(idx, size, n):
    out = jnp.zeros_like(idx)
    for h in range(1, n):
        out = out + jnp.where(idx >= h * size, 1, 0)
    return out


def _dot_nt(a, b):
    return lax.dot_general(a, b, (((1,), (1,)), ((), ())), preferred_element_type=F32)


def _rope_table_kernel(cos_ref, sin_ref, *, pos0):
    t, hd = cos_ref.shape
    half = hd // 2
    lane = lax.broadcasted_iota(jnp.int32, (t, hd), 1)
    row = lax.broadcasted_iota(jnp.int32, (t, hd), 0)
    j = jnp.where(lane < half, lane, lane - half).astype(F32)
    inv = jnp.exp(j * (-2.0 * math.log(ROPE_THETA) / hd))
    ang = (row + pos0).astype(F32) * inv
    cos_ref[...] = jnp.cos(ang)
    s = jnp.sin(ang)
    sin_ref[...] = jnp.where(lane < half, -s, s)


def _rope_tables(t, hd, pos0):
    return pl.pallas_call(
        functools.partial(_rope_table_kernel, pos0=pos0),
        out_shape=(jax.ShapeDtypeStruct((t, hd), F32), jax.ShapeDtypeStruct((t, hd), F32)),
        name="rope_tables",
    )()


def _head_rows(ref, h, n_tok, n_heads):
    return (slice(None),) * (len(ref.shape) - 2) + (pl.ds(h, n_tok, stride=n_heads), slice(None))


def _rope_head(a, h, cos, sin, bb, tt, hd):
    slab = a[:, h * hd:(h + 1) * hd]
    rot = pltpu.roll(slab, hd // 2, 1)
    return slab.reshape(bb, tt, hd) * cos[None] + rot.reshape(bb, tt, hd) * sin[None]


def _ada_kernel(c_ref, w_ref, b_ref, o_ref):
    h = _silu(c_ref[...]).astype(BF16)
    o_ref[...] = _dot(h, w_ref[...].astype(BF16)) + b_ref[...]


def _ada_mods(c, w, b):
    n_l, d, n = w.shape
    r = c.shape[0]
    bn = 1024
    return pl.pallas_call(
        _ada_kernel,
        grid=(n_l, n // bn),
        in_specs=[
            pl.BlockSpec((r, d), lambda l, j: (0, 0)),
            pl.BlockSpec((None, d, bn), lambda l, j: (l, 0, j)),
            pl.BlockSpec((None, 1, bn), lambda l, j: (l, 0, j)),
        ],
        out_specs=pl.BlockSpec((None, r, bn), lambda l, j: (l, 0, j)),
        out_shape=jax.ShapeDtypeStruct((n_l, r, n), F32),
        compiler_params=_params(2),
        name="ada_mods",
    )(c, w, b.reshape(n_l, 1, n))


def _pool_kernel(x_ref, mod_ref, hist_ref, gpre_ref, gpost_ref, win_ref, wgrp_ref, pscale_ref, wout_ref,
                 y_ref, hist_out_ref, z_scr, *, pos0):
    t = pl.program_id(1)
    n_t = pl.num_programs(1)
    bb, tt, d = x_ref.shape
    w = z_scr.shape[-1]
    n_hist = hist_ref.shape[1]
    m = bb * tt
    grp = w // len(POOL_WINDOWS)

    @pl.when(t == 0)
    def _():
        z_scr[:, HIST_PAD - n_hist:HIST_PAD, :] = hist_ref[...]

    @pl.when(t > 0)
    def _():
        z_scr[:, 0:HIST_PAD, :] = z_scr[:, tt:tt + HIST_PAD, :]

    mod = mod_ref[...]
    shift, scale, gate = mod[:, :, 0:d], mod[:, :, d:2 * d], mod[:, :, 2 * d:3 * d]
    x = x_ref[...]
    h = _rms(x, gpre_ref[...]) * (1.0 + scale) + shift
    ug = _dot(h.reshape(m, d).astype(BF16), win_ref[...])
    u = ug[:, 0:w]
    z_scr[:, HIST_PAD:HIST_PAD + tt, :] = u.reshape(bb, tt, w)

    pos = pos0 + t * tt + lax.broadcasted_iota(jnp.int32, (1, tt, 1), 1)
    ys = []
    for g, win in enumerate(POOL_WINDOWS):
        sl = slice(g * grp, (g + 1) * grp)
        acc = z_scr[:, HIST_PAD:HIST_PAD + tt, sl]
        for k in range(1, win):
            acc = acc + z_scr[:, HIST_PAD - k:HIST_PAD - k + tt, sl]
        cnt = jnp.minimum(pos + 1, win).astype(F32)
        dm = acc / cnt - z_scr[:, HIST_PAD:HIST_PAD + tt, sl]
        ys.append(_dot(dm.reshape(m, grp).astype(BF16), wgrp_ref[g]))
    y = jnp.concatenate(ys, axis=-1) * pscale_ref[...]
    a = (y * _silu(ug[:, w:2 * w])).astype(BF16)
    o = _dot(a, wout_ref[...]).reshape(bb, tt, d)
    y_ref[...] = x + gate * _rms(o, gpost_ref[...])

    @pl.when(t == n_t - 1)
    def _():
        hist_out_ref[...] = z_scr[:, HIST_PAD + tt - n_hist:HIST_PAD + tt, :]


def _pool_layer(x, mod, hist, g_pre, g_post, w_in, w_grp, pool_scale, w_out, pos0, bb, tt):
    b, t, d = x.shape
    w = w_out.shape[0]
    n_hist = hist.shape[1]
    n_g, grp, _ = w_grp.shape
    const2 = lambda i, j: (0, 0)
    return pl.pallas_call(
        functools.partial(_pool_kernel, pos0=pos0),
        grid=(b // bb, t // tt),
        in_specs=[
            pl.BlockSpec((bb, tt, d), lambda i, j: (i, j, 0)),
            pl.BlockSpec((bb, 1, 3 * d), lambda i, j: (i, 0, 0)),
            pl.BlockSpec((bb, n_hist, w), lambda i, j: (i, 0, 0)),
            pl.BlockSpec((1, d), const2),
            pl.BlockSpec((1, d), const2),
            pl.BlockSpec((d, 2 * w), const2),
            pl.BlockSpec((n_g, grp, grp), lambda i, j: (0, 0, 0)),
            pl.BlockSpec((1, w), const2),
            pl.BlockSpec((w, d), const2),
        ],
        out_specs=(
            pl.BlockSpec((bb, tt, d), lambda i, j: (i, j, 0)),
            pl.BlockSpec((bb, n_hist, w), lambda i, j: (i, 0, 0)),
        ),
        out_shape=(jax.ShapeDtypeStruct((b, t, d), F32), jax.ShapeDtypeStruct((b, n_hist, w), F32)),
        scratch_shapes=[pltpu.VMEM((bb, HIST_PAD + tt, w), F32)],
        compiler_params=_params(2),
        name="pool_layer",
    )(x, mod, hist, g_pre.reshape(1, d), g_post.reshape(1, d), w_in, w_grp, pool_scale.reshape(1, w), w_out)


def _normed(x_ref, mod_ref, g_ref):
    bb, tt, d = x_ref.shape
    mod = mod_ref[...]
    shift, scale = mod[:, :, 0:d], mod[:, :, d:2 * d]
    h = _rms(x_ref[...], g_ref[...]) * (1.0 + scale) + shift
    return h.reshape(bb * tt, d).astype(BF16)


def _kv_kernel(x_ref, mod_ref, g_ref, w_ref, cos_ref, sin_ref, k_ref, v_ref, *extra, n_heads, hd):
    bb, tt, _ = x_ref.shape
    kvw = n_heads * hd
    kv = _dot(_normed(x_ref, mod_ref, g_ref), w_ref[...])
    cos, sin = cos_ref[...], sin_ref[...]
    n_blk = tt // MOBA_BLOCK
    for h in range(n_heads):
        k_h = _rope_head(kv, h, cos, sin, bb, tt, hd)
        v_h = kv[:, kvw + h * hd:kvw + (h + 1) * hd]
        k_ref[_head_rows(k_ref, h, tt, n_heads)] = k_h
        v_ref[_head_rows(v_ref, h, tt, n_heads)] = v_h.reshape(bb, tt, hd)
        if extra:
            kb_ref, vt_ref, kmean_ref = extra
            kb_ref[:, :, h * hd:(h + 1) * hd] = k_h.astype(BF16)
            kmean_ref[:, h * hd:(h + 1) * hd] = jnp.mean(k_h.reshape(n_blk, MOBA_BLOCK, hd), axis=1)
            for j in range(n_blk):
                vt_ref[h, j] = v_h[j * MOBA_BLOCK:(j + 1) * MOBA_BLOCK, :].T.astype(BF16)


def _kv_proj(x, mod, g, w, cos, sin, n_heads, hd, bb, tt, with_attn_layouts):
    b, t, d = x.shape
    kvw = n_heads * hd
    const2 = lambda i, j: (0, 0)
    tile = lambda i, j: (i, j, 0)
    out_specs = [pl.BlockSpec((bb, tt * n_heads, hd), tile), pl.BlockSpec((bb, tt * n_heads, hd), tile)]
    out_shape = [jax.ShapeDtypeStruct((b, t * n_heads, hd), F32), jax.ShapeDtypeStruct((b, t * n_heads, hd), F32)]
    if with_attn_layouts:
        assert bb == 1 and tt % MOBA_BLOCK == 0
        n_blk = tt // MOBA_BLOCK
        out_specs += [
            pl.BlockSpec((bb, tt, kvw), tile),
            pl.BlockSpec((None, n_heads, n_blk, hd, MOBA_BLOCK), lambda i, j: (i, 0, j, 0, 0)),
            pl.BlockSpec((None, None, n_blk, kvw), lambda i, j: (i, j, 0, 0)),
        ]
        out_shape += [
            jax.ShapeDtypeStruct((b, t, kvw), BF16),
            jax.ShapeDtypeStruct((b, n_heads, t // MOBA_BLOCK, hd, MOBA_BLOCK), BF16),
            jax.ShapeDtypeStruct((b, t // tt, n_blk, kvw), F32),
        ]
    return pl.pallas_call(
        functools.partial(_kv_kernel, n_heads=n_heads, hd=hd),
        grid=(b // bb, t // tt),
        in_specs=[
            pl.BlockSpec((bb, tt, d), tile),
            pl.BlockSpec((bb, 1, mod.shape[-1]), lambda i, j: (i, 0, 0)),
            pl.BlockSpec((1, d), const2),
            pl.BlockSpec((d, 2 * kvw), const2),
            pl.BlockSpec((tt, hd), lambda i, j: (j, 0)),
            pl.BlockSpec((tt, hd), lambda i, j: (j, 0)),
        ],
        out_specs=tuple(out_specs),
        out_shape=tuple(out_shape),
        compiler_params=_params(2),
        name="kv_proj",
    )(x, mod, g.reshape(1, d), w, cos, sin)


def _q_kernel(x_ref, mod_ref, g_ref, w_ref, cos_ref, sin_ref, q_ref, gate_ref, *, n_heads, hd, head_split):
    bb, tt, _ = x_ref.shape
    aw = n_heads * hd
    qg = _dot(_normed(x_ref, mod_ref, g_ref), w_ref[...])
    cos, sin = cos_ref[...], sin_ref[...]
    for h in range(n_heads):
        q_h = _rope_head(qg, h, cos, sin, bb, tt, hd)
        if head_split:
            q_ref[_head_rows(q_ref, h, tt, n_heads)] = q_h
        else:
            q_ref[:, :, h * hd:(h + 1) * hd] = q_h.astype(BF16)
    gate_ref[...] = qg[:, aw:2 * aw].reshape(bb, tt, aw)


def _q_proj(x, mod, g, w, cos, sin, n_heads, hd, bb, tt, head_split):
    b, t, d = x.shape
    aw = n_heads * hd
    const2 = lambda i, j: (0, 0)
    tile = lambda i, j: (i, j, 0)
    if head_split:
        q_spec = pl.BlockSpec((bb, tt * n_heads, hd), tile)
        q_shape = jax.ShapeDtypeStruct((b, t * n_heads, hd), F32)
    else:
        q_spec = pl.BlockSpec((bb, tt, aw), tile)
        q_shape = jax.ShapeDtypeStruct((b, t, aw), BF16)
    return pl.pallas_call(
        functools.partial(_q_kernel, n_heads=n_heads, hd=hd, head_split=head_split),
        grid=(b // bb, t // tt),
        in_specs=[
            pl.BlockSpec((bb, tt, d), tile),
            pl.BlockSpec((bb, 1, mod.shape[-1]), lambda i, j: (i, 0, 0)),
            pl.BlockSpec((1, d), const2),
            pl.BlockSpec((d, 2 * aw), const2),
            pl.BlockSpec((tt, hd), lambda i, j: (j, 0)),
            pl.BlockSpec((tt, hd), lambda i, j: (j, 0)),
        ],
        out_specs=(q_spec, pl.BlockSpec((bb, tt, aw), tile)),
        out_shape=(q_shape, jax.ShapeDtypeStruct((b, t, aw), F32)),
        compiler_params=_params(2),
        name="q_proj",
    )(x, mod, g.reshape(1, d), w, cos, sin)


def _out_kernel(o_ref, gate_ref, x_ref, mod_ref, gpost_ref, w_ref, y_ref, *, head_split):
    bb, tt, d = x_ref.shape
    aw = gate_ref.shape[-1]
    sg = _silu(gate_ref[...])
    if head_split:
        hd = o_ref.shape[2]
        n_heads = aw // hd
        r = jnp.zeros((bb * tt, d), F32)
        for h in range(n_heads):
            o_h = o_ref[_head_rows(o_ref, h, tt, n_heads)]
            a_h = (o_h * sg[:, :, h * hd:(h + 1) * hd]).reshape(bb * tt, hd).astype(BF16)
            r = r + _dot(a_h, w_ref[h * hd:(h + 1) * hd, :])
    else:
        a = (o_ref[...] * sg).reshape(bb * tt, aw).astype(BF16)
        r = _dot(a, w_ref[...])
    res_gate = mod_ref[...][:, :, 2 * d:3 * d]
    y_ref[...] = x_ref[...] + res_gate * _rms(r.reshape(bb, tt, d), gpost_ref[...])


def _out_proj(o, gate, x, mod, g_post, w, bb, tt, head_split):
    b, t, d = x.shape
    aw = gate.shape[-1]
    tile = lambda i, j: (i, j, 0)
    if head_split:
        o_spec = pl.BlockSpec((bb, o.shape[1] // t * tt, o.shape[2]), tile)
    else:
        o_spec = pl.BlockSpec((bb, tt, aw), tile)
    return pl.pallas_call(
        functools.partial(_out_kernel, head_split=head_split),
        grid=(b // bb, t // tt),
        in_specs=[
            o_spec,
            pl.BlockSpec((bb, tt, aw), tile),
            pl.BlockSpec((bb, tt, d), tile),
            pl.BlockSpec((bb, 1, mod.shape[-1]), lambda i, j: (i, 0, 0)),
            pl.BlockSpec((1, d), lambda i, j: (0, 0)),
            pl.BlockSpec((aw, d), lambda i, j: (0, 0)),
        ],
        out_specs=pl.BlockSpec((bb, tt, d), tile),
        out_shape=jax.ShapeDtypeStruct((b, t, d), F32),
        compiler_params=_params(2),
        name="out_proj",
    )(o, gate, x, mod, g_post.reshape(1, d), w)


def _select_blocks_t(gate_t, n_past):
    nb = gate_t.shape[0]
    blk = lax.broadcasted_iota(jnp.int32, gate_t.shape, 0)
    rank = jnp.zeros(gate_t.shape, jnp.int32)
    for m in range(nb):
        gm = gate_t[m:m + 1, :]
        ahead = jnp.where(gm > gate_t, 1, jnp.where(gm == gate_t, jnp.where(m < blk, 1, 0), 0))
        rank = rank + jnp.where(m < n_past, ahead, 0)
    return jnp.where(blk < n_past, jnp.where(rank < MOBA_TOPK, 1, 0), 0)


def _moba_prompt_kernel(q_ref, k_ref, vt_ref, kmean_ref, o_ref, shift_scr, s_scr, *, scale, hd):
    i = pl.program_id(2)
    g_heads = q_ref.shape[1] // hd
    c = scale * LOG2E
    start = pl.multiple_of(i * MOBA_BLOCK, MOBA_BLOCK)
    key_i = lax.broadcasted_iota(jnp.int32, (MOBA_BLOCK, MOBA_BLOCK), 0)
    qry_i = lax.broadcasted_iota(jnp.int32, (MOBA_BLOCK, MOBA_BLOCK), 1)
    causal = key_i <= qry_i

    q_ts, s_owns, m0s = [], [], []
    for g in range(g_heads):
        ln = slice(g * hd, (g + 1) * hd)
        q_t = q_ref[:, ln].astype(F32).T.astype(BF16)
        gate_t = _dot(kmean_ref[:, ln].astype(BF16), q_t)
        keep = _select_blocks_t(gate_t, i)
        shift_scr[g] = jnp.where(keep > 0, 0.0, NEG_INF)
        s = jnp.where(causal, _dot(k_ref[pl.ds(start, MOBA_BLOCK), ln], q_t), NEG_INF)
        q_ts.append(q_t)
        s_owns.append(s)
        m0s.append(jnp.max(s, axis=0, keepdims=True))

    def scores_body(n, ms):
        st = pl.multiple_of(n * MOBA_BLOCK, MOBA_BLOCK)
        out = []
        for g in range(g_heads):
            sn = _dot(k_ref[pl.ds(st, MOBA_BLOCK), g * hd:(g + 1) * hd], q_ts[g])
            s_scr[g, n] = sn
            out.append(jnp.maximum(ms[g], jnp.max(sn, axis=0, keepdims=True) + shift_scr[g, pl.ds(n, 1), :]))
        return tuple(out)

    ms = lax.fori_loop(0, i, scores_body, tuple(m0s))

    carry0 = []
    for g in range(g_heads):
        p = jnp.exp2((s_owns[g] - ms[g]) * c)
        carry0 += [jnp.sum(p, axis=0, keepdims=True), _dot(vt_ref[g, i], p.astype(BF16))]

    def values_body(n, carry):
        out = []
        for g in range(g_heads):
            l_run, acc = carry[2 * g:2 * g + 2]
            pn = jnp.exp2((s_scr[g, n] - (ms[g] - shift_scr[g, pl.ds(n, 1), :])) * c)
            out += [l_run + jnp.sum(pn, axis=0, keepdims=True), acc + _dot(vt_ref[g, n], pn.astype(BF16))]
        return tuple(out)

    fin = lax.fori_loop(0, i, values_body, tuple(carry0))
    for g in range(g_heads):
        o_ref[:, g * hd:(g + 1) * hd] = (fin[2 * g + 1] / fin[2 * g]).T


def _moba_prompt(q, kb, vt, kmean, n_heads, hd, g_heads):
    b, s, aw = q.shape
    n_blk = s // MOBA_BLOCK
    gw = g_heads * hd
    return pl.pallas_call(
        functools.partial(_moba_prompt_kernel, scale=hd ** -0.5, hd=hd),
        grid=(b, n_heads // g_heads, n_blk),
        in_specs=[
            pl.BlockSpec((None, MOBA_BLOCK, gw), lambda bi, h, i: (bi, i, h)),
            pl.BlockSpec((None, s, gw), lambda bi, h, i: (bi, 0, h)),
            pl.BlockSpec((None, g_heads, n_blk, hd, MOBA_BLOCK), lambda bi, h, i: (bi, h, 0, 0, 0)),
            pl.BlockSpec((None, n_blk, gw), lambda bi, h, i: (bi, 0, h)),
        ],
        out_specs=pl.BlockSpec((None, MOBA_BLOCK, gw), lambda bi, h, i: (bi, i, h)),
        out_shape=jax.ShapeDtypeStruct((b, s, aw), F32),
        scratch_shapes=[
            pltpu.VMEM((g_heads, n_blk, MOBA_BLOCK), F32),
            pltpu.VMEM((g_heads, n_blk, MOBA_BLOCK, MOBA_BLOCK), F32),
        ],
        compiler_params=_params(3),
        name="moba_prompt",
    )(q, kb, vt, kmean)


def _moba_sample_kernel(pt_ref, q_ref, kn_ref, vn_ref, fold_ref, *refs, scale, n_pages, pps, n_heads):
    del pt_ref
    k_refs, v_refs, o_ref = refs[0:pps], refs[pps:2 * pps], refs[2 * pps]
    kmean_scr, s_scr, bmax_scr, shift_scr, l_scr, acc_scr = refs[2 * pps + 1:]
    j = pl.program_id(1)
    ks = n_pages // pps
    rows, hd = q_ref.shape
    t_new = rows // n_heads
    pr = k_refs[0].shape[0]
    page = pr // n_heads
    ppb = MOBA_BLOCK // page
    n_blocks = n_pages // ppb
    bps = pps // ppb
    lanes = kmean_scr.shape[0]
    c = scale * LOG2E
    q_all = q_ref[...].astype(BF16)

    r_i = lax.broadcasted_iota(jnp.int32, (rows, 1), 0)
    c_i = lax.broadcasted_iota(jnp.int32, (1, lanes), 1)
    r_tok = _bucket(r_i, n_heads, t_new)
    c_grp = _bucket(c_i, n_heads, lanes // n_heads)
    same_head = (r_i - n_heads * r_tok) == (c_i - n_heads * c_grp)

    def page_wide(tile):
        return jnp.concatenate([tile] * (pr // lanes), axis=1)

    @pl.when(j == 0)
    def _():
        kmean_scr[...] = jnp.zeros_like(kmean_scr)

    @pl.when(j < ks)
    def _():
        head_bias = page_wide(jnp.where(same_head, 0.0, NEG_INF))
        sums, maxes = [], []
        for a in range(pps):
            kp = k_refs[a][...]
            sums.append(jnp.sum(kp.reshape(page, n_heads, hd), axis=0))
            s = _dot_nt(q_all, kp.astype(BF16))
            s_scr[j * pps + a] = s
            maxes.append(jnp.max(s + head_bias, axis=-1, keepdims=True))
        for b in range(bps):
            blk_sum, blk_max = sums[b * ppb], maxes[b * ppb]
            for e in range(1, ppb):
                blk_sum = blk_sum + sums[b * ppb + e]
                blk_max = jnp.maximum(blk_max, maxes[b * ppb + e])
            r0 = pl.multiple_of((j * bps + b) * n_heads, n_heads)
            kmean_scr[pl.ds(r0, n_heads), :] = blk_sum * (1.0 / MOBA_BLOCK)
            bmax_scr[j * bps + b] = jnp.broadcast_to(blk_max, (rows, lanes))

    @pl.when(j == ks - 1)
    def _():
        gate = _dot_nt(q_all, kmean_scr[...].astype(BF16))
        valid = jnp.where(c_i < n_blocks * n_heads, jnp.where(same_head, 1, 0), 0)
        g = jnp.where(valid > 0, gate, NEG_INF)
        rank = jnp.zeros((rows, lanes), jnp.int32)
        for kk in range(1, lanes // n_heads):
            gr = pltpu.roll(g, kk * n_heads, 1)
            tie = jnp.where(c_i >= kk * n_heads, 1, 0)
            rank = rank + jnp.where(gr > g, 1, jnp.where(gr == g, tie, 0))
        keep = jnp.where(valid > 0, jnp.where(rank < MOBA_TOPK, 1.0, 0.0), 0.0)
        keep_blk = _dot(keep.astype(BF16), fold_ref[...])

        pad = jnp.zeros((lanes - rows, hd), F32)
        kn = jnp.concatenate([kn_ref[...], pad], axis=0).astype(BF16)
        vn = jnp.concatenate([vn_ref[...], pad], axis=0).astype(BF16)
        own_ok = jnp.where(c_i < rows, jnp.where(same_head, jnp.where(c_grp <= r_tok, 1, 0), 0), 0)
        s_own = jnp.where(own_ok > 0, _dot_nt(q_all, kn), NEG_INF)

        kept = [keep_blk[:, n:n + 1] > 0.5 for n in range(n_blocks)]
        m_fin = jnp.max(s_own, axis=-1, keepdims=True)
        for n in range(n_blocks):
            m_fin = jnp.maximum(m_fin, jnp.where(kept[n], bmax_scr[n][:, 0:1], NEG_INF))
        for n in range(n_blocks):
            shift_scr[n] = jnp.where(same_head, jnp.where(kept[n], -m_fin, NEG_INF), NEG_INF)
        p_own = jnp.exp2((s_own - m_fin) * c)
        l_scr[...] = jnp.broadcast_to(jnp.sum(p_own, axis=-1, keepdims=True), l_scr.shape)
        acc_scr[...] = _dot(p_own.astype(BF16), vn)

    @pl.when(j >= ks)
    def _():
        acc = acc_scr[...]
        l_add = jnp.zeros((rows, 1), F32)
        for b in range(bps):
            shift = page_wide(shift_scr[(j - ks) * bps + b])
            for e in range(ppb):
                a = b * ppb + e
                pn = jnp.exp2((s_scr[(j - ks) * pps + a] + shift) * c)
                l_add = l_add + jnp.sum(pn, axis=-1, keepdims=True)
                acc = acc + _dot(pn.astype(BF16), v_refs[a][...].astype(BF16))
        acc_scr[...] = acc
        l_scr[...] = l_scr[...] + l_add

    @pl.when(j == 2 * ks - 1)
    def _():
        o_ref[...] = acc_scr[...] / l_scr[...]


def _moba_sample(q, k_new, v_new, cache_k, cache_v, page_table, fold):
    n_phys, page, n_heads, hd = cache_k.shape
    db, t_new = q.shape[0], q.shape[1] // n_heads
    cache_k = cache_k.reshape(n_phys, page * n_heads, hd)
    cache_v = cache_v.reshape(n_phys, page * n_heads, hd)
    n_pages = page_table.shape[1]
    pps = 4
    assert MOBA_BLOCK % page == 0 and pps % (MOBA_BLOCK // page) == 0 and n_pages % pps == 0
    ks = n_pages // pps
    rows = t_new * n_heads
    lanes = V7X_LANES
    assert rows <= lanes and (n_pages * page // MOBA_BLOCK) * n_heads <= lanes and lanes % n_heads == 0
    assert (page * n_heads) % lanes == 0

    def k_page(a):
        return lambda s, j, pt: (pt[s, jnp.minimum(j, ks - 1) * pps + a], 0, 0)

    def v_page(a):
        return lambda s, j, pt: (pt[s, jnp.maximum(j - ks, 0) * pps + a], 0, 0)

    seq = lambda s, j, pt: (s, 0, 0)
    new_spec = pl.BlockSpec((None, rows, hd), seq)
    page_block = (None, page * n_heads, hd)
    grid_spec = pltpu.PrefetchScalarGridSpec(
        num_scalar_prefetch=1,
        grid=(db, 2 * ks),
        in_specs=[new_spec, new_spec, new_spec, pl.BlockSpec(fold.shape, lambda s, j, pt: (0, 0))]
        + [pl.BlockSpec(page_block, k_page(a)) for a in range(pps)]
        + [pl.BlockSpec(page_block, v_page(a)) for a in range(pps)],
        out_specs=new_spec,
        scratch_shapes=[
            pltpu.VMEM((lanes, hd), F32),
            pltpu.VMEM((n_pages, rows, page * n_heads), F32),
            pltpu.VMEM((n_pages * page // MOBA_BLOCK, rows, lanes), F32),
            pltpu.VMEM((n_pages * page // MOBA_BLOCK, rows, lanes), F32),
            pltpu.VMEM((rows, hd), F32),
            pltpu.VMEM((rows, hd), F32),
        ],
    )
    return pl.pallas_call(
        functools.partial(_moba_sample_kernel, scale=hd ** -0.5, n_pages=n_pages, pps=pps, n_heads=n_heads),
        grid_spec=grid_spec,
        out_shape=jax.ShapeDtypeStruct((db, rows, hd), F32),
        compiler_params=_params(2),
        name="moba_sample",
    )(page_table, q, k_new, v_new, fold, *([cache_k] * pps), *([cache_v] * pps))


def _fold_matrix(n_blocks, n_heads, lanes):
    r = jnp.arange(lanes)[:, None]
    c = jnp.arange(lanes)[None, :]
    return ((r // n_heads == c) & (r < n_blocks * n_heads)).astype(BF16)


def _trunk(x, mods, mods_kv, hist, pos0, wts, n_heads, hd, bb, tt, attend, is_prompt):
    t = x.shape[1]
    n_a = wts["w_in_pool"].shape[0]
    n_b = wts["w_in_attn"].shape[0]
    cos, sin = _rope_tables(t, hd, pos0)
    new_hist = []
    for l in range(n_a):
        x, hn = _pool_layer(x, mods[l], hist[l], wts["g_pre"][l], wts["g_post"][l], wts["w_in_pool"][l],
                            wts["w_pool_group"][l], wts["pool_scale"][l], wts["w_out_pool"][l], pos0, bb, tt)
        new_hist.append(hn)
    kv = _kv_proj(x, mods_kv, wts["g_kv"], wts["w_kv"], cos, sin, n_heads, hd, bb, tt, is_prompt)
    for jl in range(n_b):
        l = n_a + jl
        q, gate = _q_proj(x, mods[l], wts["g_pre"][l], wts["w_in_attn"][jl], cos, sin, n_heads, hd, bb, tt,
                          head_split=not is_prompt)
        o = attend(q, kv)
        x = _out_proj(o, gate, x, mods[l], wts["g_post"][l], wts["w_out_attn"][jl], bb, tt,
                      head_split=not is_prompt)
    b = x.shape[0]
    return x, kv[0].reshape(b, t, n_heads, hd), kv[1].reshape(b, t, n_heads, hd), jnp.stack(new_hist)


def kernel(x_prompt, x_sample, cache_k, cache_v, state_pool, page_table, c_prompt, c_sample, w_ada, b_ada,
           g_pre, g_post, w_in_pool, w_pool_group, pool_scale, w_out_pool, g_kv, w_ada_kv, b_ada_kv, w_kv,
           w_in_attn, w_out_attn):
    n_b_p = x_prompt.shape[0]
    page, n_heads, hd = cache_k.shape[1], cache_k.shape[2], cache_k.shape[3]
    past_len = page_table.shape[1] * page
    n_a = w_in_pool.shape[0]
    n_hist = state_pool.shape[2]

    c_all = jnp.concatenate([c_prompt, c_sample], axis=0)
    r = c_all.shape[0]
    r_pad = -(-r // V7X_SUBLANES) * V7X_SUBLANES
    c_all = jnp.pad(c_all, ((0, r_pad - r), (0, 0)))
    mods_all = _ada_mods(c_all, w_ada, b_ada)
    mods_kv_all = _ada_mods(c_all, w_ada_kv[None], b_ada_kv[None])[0]
    mods_p = [mods_all[l, 0:n_b_p][:, None, :] for l in range(mods_all.shape[0])]
    mods_s = [mods_all[l, n_b_p:r][:, None, :] for l in range(mods_all.shape[0])]
    mods_kv_p = mods_kv_all[0:n_b_p][:, None, :]
    mods_kv_s = mods_kv_all[n_b_p:r][:, None, :]

    wts = dict(
        g_pre=g_pre, g_post=g_post, pool_scale=pool_scale, g_kv=g_kv,
        w_in_pool=w_in_pool.astype(BF16), w_pool_group=w_pool_group.astype(BF16),
        w_out_pool=w_out_pool.astype(BF16), w_kv=w_kv.astype(BF16),
        w_in_attn=w_in_attn.astype(BF16), w_out_attn=w_out_attn.astype(BF16))

    def attend_prompt(q, kv):
        _, _, kb, vt, kmean = kv
        kmean = kmean.reshape(kmean.shape[0], -1, kmean.shape[-1])
        return _moba_prompt(q, kb, vt, kmean, n_heads, hd, g_heads=4)

    fold = _fold_matrix(past_len // MOBA_BLOCK, n_heads, V7X_LANES)

    def attend_sample(q, kv):
        return _moba_sample(q, kv[0], kv[1], cache_k, cache_v, page_table, fold)

    hist0 = jnp.zeros((n_a, n_b_p, n_hist, state_pool.shape[3]), F32)
    y_p, k_p, v_p, pool_p = _trunk(x_prompt, mods_p, mods_kv_p, hist0, 0, wts, n_heads, hd,
                                   bb=1, tt=512, attend=attend_prompt, is_prompt=True)
    y_s, k_s, v_s, pool_s = _trunk(x_sample, mods_s, mods_kv_s, state_pool, past_len, wts, n_heads, hd,
                                   bb=16, tt=x_sample.shape[1], attend=attend_sample, is_prompt=False)
    return (y_p, y_s, k_p, v_p, k_s, v_s, pool_p, pool_s)
```

```python
import functools
import math

import jax
import jax.numpy as jnp
from jax import lax
from jax.experimental import pallas as pl
from jax.experimental.pallas import tpu as pltpu

F32 = jnp.float32
BF16 = jnp.bfloat16
NEG_INF = float("-inf")
LOG2E = math.log2(math.e)

POOL_WINDOWS = (2, 4, 8, 16)
MOBA_BLOCK = 256
MOBA_TOPK = 3
ROPE_THETA = 10000.0
NORM_EPS = 1e-6

V7X_LANES = 128
V7X_SUBLANES = 8
V7X_VMEM_LIMIT_BYTES = 56 * 1024 * 1024

HIST_PAD = 16
PROMPT_HEADS_PER_STEP = 4
PROMPT_BLOCKS_PER_ITER = 2
SAMPLE_PAGES_PER_STEP = 4
SAMPLE_RING_SLOTS = 4


def _params(n_axes):
    return pltpu.CompilerParams(
        dimension_semantics=("arbitrary",) * n_axes, vmem_limit_bytes=V7X_VMEM_LIMIT_BYTES)


def _silu(x):
    return x * (1.0 / (1.0 + jnp.exp(-x)))


def _rms(x, g):
    return x * lax.rsqrt(jnp.mean(x * x, axis=-1, keepdims=True) + NORM_EPS) * g


def _dot(a, b):
    return jnp.dot(a, b, preferred_element_type=F32)


def _bucket(idx, size, n):
    out = jnp.zeros_like(idx)
    for h in range(1, n):
        out = out + jnp.where(idx >= h * size, 1, 0)
    return out


def _dot_nt(a, b):
    return lax.dot_general(a, b, (((1,), (1,)), ((), ())), preferred_element_type=F32)


def _rope_table_kernel(cos_ref, sin_ref, *, pos0):
    t, hd = cos_ref.shape
    half = hd // 2
    lane = lax.broadcasted_iota(jnp.int32, (t, hd), 1)
    row = lax.broadcasted_iota(jnp.int32, (t, hd), 0)
    j = jnp.where(lane < half, lane, lane - half).astype(F32)
    inv = jnp.exp(j * (-2.0 * math.log(ROPE_THETA) / hd))
    ang = (row + pos0).astype(F32) * inv
    cos_ref[...] = jnp.cos(ang)
    s = jnp.sin(ang)
    sin_ref[...] = jnp.where(lane < half, -s, s)


def _rope_tables(t, hd, pos0):
    return pl.pallas_call(
        functools.partial(_rope_table_kernel, pos0=pos0),
        out_shape=(jax.ShapeDtypeStruct((t, hd), F32), jax.ShapeDtypeStruct((t, hd), F32)),
        name="rope_tables",
    )()


def _head_rows(ref, h, n_tok, n_heads):
    return (slice(None),) * (len(ref.shape) - 2) + (pl.ds(h, n_tok, stride=n_heads), slice(None))


def _rope_head(a, h, cos, sin, bb, tt, hd):
    slab = a[:, h * hd:(h + 1) * hd]
    rot = pltpu.roll(slab, hd // 2, 1)
    return slab.reshape(bb, tt, hd) * cos[None] + rot.reshape(bb, tt, hd) * sin[None]


def _ada_kernel(c_ref, w_ref, b_ref, o_ref):
    h = _silu(c_ref[...]).astype(BF16)
    o_ref[...] = _dot(h, w_ref[...].astype(BF16)) + b_ref[...]


def _ada_mods(c, w, b):
    n_l, d, n = w.shape
    r = c.shape[0]
    bn = 1024
    return pl.pallas_call(
        _ada_kernel,
        grid=(n_l, n // bn),
        in_specs=[
            pl.BlockSpec((r, d), lambda l, j: (0, 0)),
            pl.BlockSpec((None, d, bn), lambda l, j: (l, 0, j)),
            pl.BlockSpec((None, 1, bn), lambda l, j: (l, 0, j)),
        ],
        out_specs=pl.BlockSpec((None, r, bn), lambda l, j: (l, 0, j)),
        out_shape=jax.ShapeDtypeStruct((n_l, r, n), F32),
        compiler_params=_params(2),
        name="ada_mods",
    )(c, w, b.reshape(n_l, 1, n))


def _pool_kernel(x_ref, mod_ref, hist_ref, gpre_ref, gpost_ref, win_ref, wgrp_ref, pscale_ref, wout_ref,
                 y_ref, hist_out_ref, z_scr, *, pos0):
    t = pl.program_id(1)
    n_t = pl.num_programs(1)
    bb, tt, d = x_ref.shape
    w = z_scr.shape[-1]
    n_hist = hist_ref.shape[1]
    m = bb * tt
    grp = w // len(POOL_WINDOWS)

    @pl.when(t == 0)
    def _():
        z_scr[:, HIST_PAD - n_hist:HIST_PAD, :] = hist_ref[...]

    @pl.when(t > 0)
    def _():
        z_scr[:, 0:HIST_PAD, :] = z_scr[:, tt:tt + HIST_PAD, :]

    mod = mod_ref[...]
    shift, scale, gate = mod[:, :, 0:d], mod[:, :, d:2 * d], mod[:, :, 2 * d:3 * d]
    x = x_ref[...]
    h = _rms(x, gpre_ref[...]) * (1.0 + scale) + shift
    ug = _dot(h.reshape(m, d).astype(BF16), win_ref[...])
    u = ug[:, 0:w]
    z_scr[:, HIST_PAD:HIST_PAD + tt, :] = u.reshape(bb, tt, w)

    pos = pos0 + t * tt + lax.broadcasted_iota(jnp.int32, (1, tt, 1), 1)
    ys = []
    for g, win in enumerate(POOL_WINDOWS):
        sl = slice(g * grp, (g + 1) * grp)
        acc = z_scr[:, HIST_PAD:HIST_PAD + tt, sl]
        for k in range(1, win):
            acc = acc + z_scr[:, HIST_PAD - k:HIST_PAD - k + tt, sl]
        cnt = jnp.minimum(pos + 1, win).astype(F32)
        dm = acc / cnt - z_scr[:, HIST_PAD:HIST_PAD + tt, sl]
        ys.append(_dot(dm.reshape(m, grp).astype(BF16), wgrp_ref[g]))
    y = jnp.concatenate(ys, axis=-1) * pscale_ref[...]
    a = (y * _silu(ug[:, w:2 * w])).astype(BF16)
    o = _dot(a, wout_ref[...]).reshape(bb, tt, d)
    y_ref[...] = x + gate * _rms(o, gpost_ref[...])

    @pl.when(t == n_t - 1)
    def _():
        hist_out_ref[...] = z_scr[:, HIST_PAD + tt - n_hist:HIST_PAD + tt, :]


def _pool_layer(x, mod, hist, g_pre, g_post, w_in, w_grp, pool_scale, w_out, pos0, bb, tt):
    b, t, d = x.shape
    w = w_out.shape[0]
    n_hist = hist.shape[1]
    n_g, grp, _ = w_grp.shape
    const2 = lambda i, j: (0, 0)
    return pl.pallas_call(
        functools.partial(_pool_kernel, pos0=pos0),
        grid=(b // bb, t // tt),
        in_specs=[
            pl.BlockSpec((bb, tt, d), lambda i, j: (i, j, 0)),
            pl.BlockSpec((bb, 1, 3 * d), lambda i, j: (i, 0, 0)),
            pl.BlockSpec((bb, n_hist, w), lambda i, j: (i, 0, 0)),
            pl.BlockSpec((1, d), const2),
            pl.BlockSpec((1, d), const2),
            pl.BlockSpec((d, 2 * w), const2),
            pl.BlockSpec((n_g, grp, grp), lambda i, j: (0, 0, 0)),
            pl.BlockSpec((1, w), const2),
            pl.BlockSpec((w, d), const2),
        ],
        out_specs=(
            pl.BlockSpec((bb, tt, d), lambda i, j: (i, j, 0)),
            pl.BlockSpec((bb, n_hist, w), lambda i, j: (i, 0, 0)),
        ),
        out_shape=(jax.ShapeDtypeStruct((b, t, d), F32), jax.ShapeDtypeStruct((b, n_hist, w), F32)),
        scratch_shapes=[pltpu.VMEM((bb, HIST_PAD + tt, w), F32)],
        compiler_params=_params(2),
        name="pool_layer",
    )(x, mod, hist, g_pre.reshape(1, d), g_post.reshape(1, d), w_in, w_grp, pool_scale.reshape(1, w), w_out)


def _normed(x_ref, mod_ref, g_ref):
    bb, tt, d = x_ref.shape
    mod = mod_ref[...]
    shift, scale = mod[:, :, 0:d], mod[:, :, d:2 * d]
    h = _rms(x_ref[...], g_ref[...]) * (1.0 + scale) + shift
    return h.reshape(bb * tt, d).astype(BF16)


def _kv_kernel(x_ref, mod_ref, g_ref, w_ref, cos_ref, sin_ref, k_ref, v_ref, *extra, n_heads, hd):
    bb, tt, _ = x_ref.shape
    kvw = n_heads * hd
    kv = _dot(_normed(x_ref, mod_ref, g_ref), w_ref[...])
    cos, sin = cos_ref[...], sin_ref[...]
    n_blk = tt // MOBA_BLOCK
    for h in range(n_heads):
        k_h = _rope_head(kv, h, cos, sin, bb, tt, hd)
        v_h = kv[:, kvw + h * hd:kvw + (h + 1) * hd]
        k_ref[_head_rows(k_ref, h, tt, n_heads)] = k_h
        v_ref[_head_rows(v_ref, h, tt, n_heads)] = v_h.reshape(bb, tt, hd)
        if extra:
            kb_ref, vt_ref, kmean_ref = extra
            kb_ref[:, :, h * hd:(h + 1) * hd] = k_h.astype(BF16)
            kmean_ref[:, h * hd:(h + 1) * hd] = jnp.mean(k_h.reshape(n_blk, MOBA_BLOCK, hd), axis=1)
            for j in range(n_blk):
                vt_ref[h, j] = v_h[j * MOBA_BLOCK:(j + 1) * MOBA_BLOCK, :].T.astype(BF16)


def _kv_proj(x, mod, g, w, cos, sin, n_heads, hd, bb, tt, with_attn_layouts):
    b, t, d = x.shape
    kvw = n_heads * hd
    const2 = lambda i, j: (0, 0)
    tile = lambda i, j: (i, j, 0)
    out_specs = [pl.BlockSpec((bb, tt * n_heads, hd), tile), pl.BlockSpec((bb, tt * n_heads, hd), tile)]
    out_shape = [jax.ShapeDtypeStruct((b, t * n_heads, hd), F32), jax.ShapeDtypeStruct((b, t * n_heads, hd), F32)]
    if with_attn_layouts:
        assert bb == 1 and tt % MOBA_BLOCK == 0
        n_blk = tt // MOBA_BLOCK
        out_specs += [
            pl.BlockSpec((bb, tt, kvw), tile),
            pl.BlockSpec((None, n_heads, n_blk, hd, MOBA_BLOCK), lambda i, j: (i, 0, j, 0, 0)),
            pl.BlockSpec((None, None, n_blk, kvw), lambda i, j: (i, j, 0, 0)),
        ]
        out_shape += [
            jax.ShapeDtypeStruct((b, t, kvw), BF16),
            jax.ShapeDtypeStruct((b, n_heads, t // MOBA_BLOCK, hd, MOBA_BLOCK), BF16),
            jax.ShapeDtypeStruct((b, t // tt, n_blk, kvw), F32),
        ]
    return pl.pallas_call(
        functools.partial(_kv_kernel, n_heads=n_heads, hd=hd),
        grid=(b // bb, t // tt),
        in_specs=[
            pl.BlockSpec((bb, tt, d), tile),
            pl.BlockSpec((bb, 1, mod.shape[-1]), lambda i, j: (i, 0, 0)),
            pl.BlockSpec((1, d), const2),
            pl.BlockSpec((d, 2 * kvw), const2),
            pl.BlockSpec((tt, hd), lambda i, j: (j, 0)),
            pl.BlockSpec((tt, hd), lambda i, j: (j, 0)),
        ],
        out_specs=tuple(out_specs),
        out_shape=tuple(out_shape),
        compiler_params=_params(2),
        name="kv_proj",
    )(x, mod, g.reshape(1, d), w, cos, sin)


def _q_kernel(x_ref, mod_ref, g_ref, w_ref, cos_ref, sin_ref, q_ref, gate_ref, *, n_heads, hd, head_split):
    bb, tt, _ = x_ref.shape
    aw = n_heads * hd
    qg = _dot(_normed(x_ref, mod_ref, g_ref), w_ref[...])
    cos, sin = cos_ref[...], sin_ref[...]
    for h in range(n_heads):
        q_h = _rope_head(qg, h, cos, sin, bb, tt, hd)
        if head_split:
            q_ref[_head_rows(q_ref, h, tt, n_heads)] = q_h
        else:
            q_ref[:, :, h * hd:(h + 1) * hd] = q_h.astype(BF16)
    gate_ref[...] = qg[:, aw:2 * aw].reshape(bb, tt, aw).astype(gate_ref.dtype)


def _q_proj(x, mod, g, w, cos, sin, n_heads, hd, bb, tt, head_split):
    b, t, d = x.shape
    aw = n_heads * hd
    const2 = lambda i, j: (0, 0)
    tile = lambda i, j: (i, j, 0)
    if head_split:
        q_spec = pl.BlockSpec((bb, tt * n_heads, hd), tile)
        q_shape = jax.ShapeDtypeStruct((b, t * n_heads, hd), F32)
    else:
        q_spec = pl.BlockSpec((bb, tt, aw), tile)
        q_shape = jax.ShapeDtypeStruct((b, t, aw), BF16)
    return pl.pallas_call(
        functools.partial(_q_kernel, n_heads=n_heads, hd=hd, head_split=head_split),
        grid=(b // bb, t // tt),
        in_specs=[
            pl.BlockSpec((bb, tt, d), tile),
            pl.BlockSpec((bb, 1, mod.shape[-1]), lambda i, j: (i, 0, 0)),
            pl.BlockSpec((1, d), const2),
            pl.BlockSpec((d, 2 * aw), const2),
            pl.BlockSpec((tt, hd), lambda i, j: (j, 0)),
            pl.BlockSpec((tt, hd), lambda i, j: (j, 0)),
        ],
        out_specs=(q_spec, pl.BlockSpec((bb, tt, aw), tile)),
        out_shape=(q_shape, jax.ShapeDtypeStruct((b, t, aw), q_shape.dtype)),
        compiler_params=_params(2),
        name="q_proj",
    )(x, mod, g.reshape(1, d), w, cos, sin)


def _out_kernel(o_ref, gate_ref, x_ref, mod_ref, gpost_ref, w_ref, y_ref, *, head_split):
    bb, tt, d = x_ref.shape
    aw = gate_ref.shape[-1]
    sg = _silu(gate_ref[...].astype(F32))
    if head_split:
        hd = o_ref.shape[2]
        n_heads = aw // hd
        r = jnp.zeros((bb * tt, d), F32)
        for h in range(n_heads):
            o_h = o_ref[_head_rows(o_ref, h, tt, n_heads)]
            a_h = (o_h * sg[:, :, h * hd:(h + 1) * hd]).reshape(bb * tt, hd).astype(BF16)
            r = r + _dot(a_h, w_ref[h * hd:(h + 1) * hd, :])
    else:
        a = (o_ref[...].astype(F32) * sg).reshape(bb * tt, aw).astype(BF16)
        r = _dot(a, w_ref[...])
    res_gate = mod_ref[...][:, :, 2 * d:3 * d]
    y_ref[...] = x_ref[...] + res_gate * _rms(r.reshape(bb, tt, d), gpost_ref[...])


def _out_proj(o, gate, x, mod, g_post, w, bb, tt, head_split):
    b, t, d = x.shape
    aw = gate.shape[-1]
    tile = lambda i, j: (i, j, 0)
    if head_split:
        o_spec = pl.BlockSpec((bb, o.shape[1] // t * tt, o.shape[2]), tile)
    else:
        o_spec = pl.BlockSpec((bb, tt, aw), tile)
    return pl.pallas_call(
        functools.partial(_out_kernel, head_split=head_split),
        grid=(b // bb, t // tt),
        in_specs=[
            o_spec,
            pl.BlockSpec((bb, tt, aw), tile),
            pl.BlockSpec((bb, tt, d), tile),
            pl.BlockSpec((bb, 1, mod.shape[-1]), lambda i, j: (i, 0, 0)),
            pl.BlockSpec((1, d), lambda i, j: (0, 0)),
            pl.BlockSpec((aw, d), lambda i, j: (0, 0)),
        ],
        out_specs=pl.BlockSpec((bb, tt, d), tile),
        out_shape=jax.ShapeDtypeStruct((b, t, d), F32),
        compiler_params=_params(2),
        name="out_proj",
    )(o, gate, x, mod, g_post.reshape(1, d), w)


def _select_blocks_t(gate_t, n_past):
    nb = gate_t.shape[0]
    blk = lax.broadcasted_iota(jnp.int32, gate_t.shape, 0)
    rank = jnp.zeros(gate_t.shape, jnp.int32)
    for m in range(nb):
        gm = gate_t[m:m + 1, :]
        ahead = jnp.where(gm > gate_t, 1, jnp.where(gm == gate_t, jnp.where(m < blk, 1, 0), 0))
        rank = rank + jnp.where(m < n_past, ahead, 0)
    return jnp.where(blk < n_past, jnp.where(rank < MOBA_TOPK, 1, 0), 0)


def _moba_prompt_kernel(q_ref, k_ref, vt_ref, kmean_ref, o_ref, shift_scr, s_scr, *, scale, hd):
    i = pl.program_id(2)
    g_heads = q_ref.shape[1] // hd
    c = scale * LOG2E
    start = pl.multiple_of(i * MOBA_BLOCK, MOBA_BLOCK)
    key_i = lax.broadcasted_iota(jnp.int32, (MOBA_BLOCK, MOBA_BLOCK), 0)
    qry_i = lax.broadcasted_iota(jnp.int32, (MOBA_BLOCK, MOBA_BLOCK), 1)
    causal = key_i <= qry_i

    q_ts, s_owns, m0s = [], [], []
    for g in range(g_heads):
        ln = slice(g * hd, (g + 1) * hd)
        q_t = q_ref[:, ln].astype(F32).T.astype(BF16)
        gate_t = _dot(kmean_ref[:, ln].astype(BF16), q_t)
        keep = _select_blocks_t(gate_t, i)
        shift_scr[g] = jnp.where(keep > 0, 0.0, NEG_INF)
        s = jnp.where(causal, _dot(k_ref[pl.ds(start, MOBA_BLOCK), ln], q_t), NEG_INF)
        q_ts.append(q_t)
        s_owns.append(s)
        m0s.append(jnp.max(s, axis=0, keepdims=True))

    per_iter = PROMPT_BLOCKS_PER_ITER
    n_full = i // per_iter
    n_rem = i - n_full * per_iter

    def over_past_blocks(blocks_fn, carry):
        carry = lax.fori_loop(0, n_full, lambda t, cr: blocks_fn(t * per_iter, per_iter, cr), carry)
        for r in range(1, per_iter):
            carry = lax.cond(n_rem >= r, lambda cr, r=r: blocks_fn(n_full * per_iter + (r - 1), 1, cr),
                             lambda cr: cr, carry)
        return carry

    def scores_blocks(n0, count, ms):
        out = list(ms)
        for u in range(count):
            n = n0 + u
            st = pl.multiple_of(n * MOBA_BLOCK, MOBA_BLOCK)
            for g in range(g_heads):
                sn = _dot(k_ref[pl.ds(st, MOBA_BLOCK), g * hd:(g + 1) * hd], q_ts[g])
                s_scr[g, n] = sn
                out[g] = jnp.maximum(out[g], jnp.max(sn, axis=0, keepdims=True) + shift_scr[g, pl.ds(n, 1), :])
        return tuple(out)

    ms = over_past_blocks(scores_blocks, tuple(m0s))

    carry0 = []
    for g in range(g_heads):
        p = jnp.exp2((s_owns[g] - ms[g]) * c)
        carry0 += [jnp.sum(p, axis=0, keepdims=True), _dot(vt_ref[g, i], p.astype(BF16))]

    def values_blocks(n0, count, carry):
        out = list(carry)
        for u in range(count):
            n = n0 + u
            for g in range(g_heads):
                pn = jnp.exp2((s_scr[g, n] - (ms[g] - shift_scr[g, pl.ds(n, 1), :])) * c)
                out[2 * g] = out[2 * g] + jnp.sum(pn, axis=0, keepdims=True)
                out[2 * g + 1] = out[2 * g + 1] + _dot(vt_ref[g, n], pn.astype(BF16))
        return tuple(out)

    fin = over_past_blocks(values_blocks, tuple(carry0))
    for g in range(g_heads):
        o_ref[:, g * hd:(g + 1) * hd] = (fin[2 * g + 1] / fin[2 * g]).T.astype(o_ref.dtype)


def _moba_prompt(q, kb, vt, kmean, n_heads, hd, g_heads):
    b, s, aw = q.shape
    n_blk = s // MOBA_BLOCK
    gw = g_heads * hd
    return pl.pallas_call(
        functools.partial(_moba_prompt_kernel, scale=hd ** -0.5, hd=hd),
        grid=(b, n_heads // g_heads, n_blk),
        in_specs=[
            pl.BlockSpec((None, MOBA_BLOCK, gw), lambda bi, h, i: (bi, i, h)),
            pl.BlockSpec((None, s, gw), lambda bi, h, i: (bi, 0, h)),
            pl.BlockSpec((None, g_heads, n_blk, hd, MOBA_BLOCK), lambda bi, h, i: (bi, h, 0, 0, 0)),
            pl.BlockSpec((None, n_blk, gw), lambda bi, h, i: (bi, 0, h)),
        ],
        out_specs=pl.BlockSpec((None, MOBA_BLOCK, gw), lambda bi, h, i: (bi, i, h)),
        out_shape=jax.ShapeDtypeStruct((b, s, aw), BF16),
        scratch_shapes=[
            pltpu.VMEM((g_heads, n_blk, MOBA_BLOCK), F32),
            pltpu.VMEM((g_heads, n_blk, MOBA_BLOCK, MOBA_BLOCK), F32),
        ],
        compiler_params=_params(3),
        name="moba_prompt",
    )(q, kb, vt, kmean)


def _moba_sample_kernel(pt_ref, q_ref, kn_ref, vn_ref, fold_ref, ck_ref, cv_ref, o_ref,
                        kmean_scr, s_scr, bmax_scr, shift_scr, l_scr, acc_scr, ring, sems,
                        *, scale, n_pages, n_heads):
    seq = pl.program_id(0)
    j = pl.program_id(1)
    n_slots, pps, pr, hd = ring.shape
    ks = n_pages // pps
    steps = 2 * ks
    look = n_slots - 1
    rows = q_ref.shape[0]
    t_new = rows // n_heads
    page = pr // n_heads

    def page_copy(cache_ref, page_id, slot, a):
        return pltpu.make_async_copy(cache_ref.at[page_id], ring.at[slot, a], sems.at[slot])

    def start_step(seq_i, j_i, slot):
        is_k = j_i < ks
        first = jnp.where(is_k, j_i, j_i - ks) * pps
        for a in range(pps):
            page_id = pt_ref[seq_i, first + a]

            @pl.when(is_k)
            def _():
                page_copy(ck_ref, page_id, slot, a).start()

            @pl.when(jnp.logical_not(is_k))
            def _():
                page_copy(cv_ref, page_id, slot, a).start()

    @pl.when(jnp.logical_and(seq == 0, j == 0))
    def _():
        for ahead in range(look):
            start_step(0, ahead, ahead)

    g = seq * steps + j
    j_next = j + look
    wrap = j_next >= steps
    seq_next = jnp.where(wrap, seq + 1, seq)

    @pl.when(seq_next < pl.num_programs(0))
    def _():
        start_step(seq_next, jnp.where(wrap, j_next - steps, j_next), lax.rem(g + look, n_slots))

    slot = lax.rem(g, n_slots)
    for a in range(pps):
        page_copy(ck_ref, 0, slot, a).wait()
    ppb = MOBA_BLOCK // page
    n_blocks = n_pages // ppb
    bps = pps // ppb
    lanes = kmean_scr.shape[0]
    c = scale * LOG2E
    q_all = q_ref[...].astype(BF16)

    r_i = lax.broadcasted_iota(jnp.int32, (rows, 1), 0)
    c_i = lax.broadcasted_iota(jnp.int32, (1, lanes), 1)
    r_tok = _bucket(r_i, n_heads, t_new)
    c_grp = _bucket(c_i, n_heads, lanes // n_heads)
    same_head = (r_i - n_heads * r_tok) == (c_i - n_heads * c_grp)

    def page_wide(tile):
        return jnp.concatenate([tile] * (pr // lanes), axis=1)

    @pl.when(j == 0)
    def _():
        kmean_scr[...] = jnp.zeros_like(kmean_scr)

    @pl.when(j < ks)
    def _():
        head_bias = page_wide(jnp.where(same_head, 0.0, NEG_INF))
        sums, maxes = [], []
        for a in range(pps):
            kp = ring[slot, a]
            sums.append(jnp.sum(kp.reshape(page, n_heads, hd), axis=0))
            s = _dot_nt(q_all, kp.astype(BF16))
            s_scr[j * pps + a] = s
            maxes.append(jnp.max(s + head_bias, axis=-1, keepdims=True))
        for b in range(bps):
            blk_sum, blk_max = sums[b * ppb], maxes[b * ppb]
            for e in range(1, ppb):
                blk_sum = blk_sum + sums[b * ppb + e]
                blk_max = jnp.maximum(blk_max, maxes[b * ppb + e])
            r0 = pl.multiple_of((j * bps + b) * n_heads, n_heads)
            kmean_scr[pl.ds(r0, n_heads), :] = blk_sum * (1.0 / MOBA_BLOCK)
            bmax_scr[j * bps + b] = jnp.broadcast_to(blk_max, (rows, lanes))

    @pl.when(j == ks - 1)
    def _():
        gate = _dot_nt(q_all, kmean_scr[...].astype(BF16))
        valid = jnp.where(c_i < n_blocks * n_heads, jnp.where(same_head, 1, 0), 0)
        g = jnp.where(valid > 0, gate, NEG_INF)
        rank = jnp.zeros((rows, lanes), jnp.int32)
        for kk in range(1, lanes // n_heads):
            gr = pltpu.roll(g, kk * n_heads, 1)
            tie = jnp.where(c_i >= kk * n_heads, 1, 0)
            rank = rank + jnp.where(gr > g, 1, jnp.where(gr == g, tie, 0))
        keep = jnp.where(valid > 0, jnp.where(rank < MOBA_TOPK, 1.0, 0.0), 0.0)
        keep_blk = _dot(keep.astype(BF16), fold_ref[...])

        pad = jnp.zeros((lanes - rows, hd), F32)
        kn = jnp.concatenate([kn_ref[...], pad], axis=0).astype(BF16)
        vn = jnp.concatenate([vn_ref[...], pad], axis=0).astype(BF16)
        own_ok = jnp.where(c_i < rows, jnp.where(same_head, jnp.where(c_grp <= r_tok, 1, 0), 0), 0)
        s_own = jnp.where(own_ok > 0, _dot_nt(q_all, kn), NEG_INF)

        kept = [keep_blk[:, n:n + 1] > 0.5 for n in range(n_blocks)]
        m_fin = jnp.max(s_own, axis=-1, keepdims=True)
        for n in range(n_blocks):
            m_fin = jnp.maximum(m_fin, jnp.where(kept[n], bmax_scr[n][:, 0:1], NEG_INF))
        for n in range(n_blocks):
            shift_scr[n] = jnp.where(same_head, jnp.where(kept[n], -m_fin, NEG_INF), NEG_INF)
        p_own = jnp.exp2((s_own - m_fin) * c)
        l_scr[...] = jnp.broadcast_to(jnp.sum(p_own, axis=-1, keepdims=True), l_scr.shape)
        acc_scr[...] = _dot(p_own.astype(BF16), vn)

    @pl.when(j >= ks)
    def _():
        acc = acc_scr[...]
        l_add = jnp.zeros((rows, 1), F32)
        for b in range(bps):
            shift = page_wide(shift_scr[(j - ks) * bps + b])
            for e in range(ppb):
                a = b * ppb + e
                pn = jnp.exp2((s_scr[(j - ks) * pps + a] + shift) * c)
                l_add = l_add + jnp.sum(pn, axis=-1, keepdims=True)
                acc = acc + _dot(pn.astype(BF16), ring[slot, a].astype(BF16))
        acc_scr[...] = acc
        l_scr[...] = l_scr[...] + l_add

    @pl.when(j == 2 * ks - 1)
    def _():
        o_ref[...] = acc_scr[...] / l_scr[...]


def _moba_sample(q, k_new, v_new, cache_k, cache_v, page_table, fold):
    n_phys, page, n_heads, hd = cache_k.shape
    db, t_new = q.shape[0], q.shape[1] // n_heads
    cache_k = cache_k.reshape(n_phys, page * n_heads, hd)
    cache_v = cache_v.reshape(n_phys, page * n_heads, hd)
    n_pages = page_table.shape[1]
    pps = SAMPLE_PAGES_PER_STEP
    n_slots = SAMPLE_RING_SLOTS
    assert MOBA_BLOCK % page == 0 and pps % (MOBA_BLOCK // page) == 0 and n_pages % pps == 0
    ks = n_pages // pps
    assert n_slots - 1 <= 2 * ks
    rows = t_new * n_heads
    lanes = V7X_LANES
    assert rows <= lanes and (n_pages * page // MOBA_BLOCK) * n_heads <= lanes and lanes % n_heads == 0
    assert (page * n_heads) % lanes == 0

    seq = lambda s, j, pt: (s, 0, 0)
    new_spec = pl.BlockSpec((None, rows, hd), seq)
    grid_spec = pltpu.PrefetchScalarGridSpec(
        num_scalar_prefetch=1,
        grid=(db, 2 * ks),
        in_specs=[new_spec, new_spec, new_spec, pl.BlockSpec(fold.shape, lambda s, j, pt: (0, 0)),
                  pl.BlockSpec(memory_space=pl.ANY), pl.BlockSpec(memory_space=pl.ANY)],
        out_specs=new_spec,
        scratch_shapes=[
            pltpu.VMEM((lanes, hd), F32),
            pltpu.VMEM((n_pages, rows, page * n_heads), F32),
            pltpu.VMEM((n_pages * page // MOBA_BLOCK, rows, lanes), F32),
            pltpu.VMEM((n_pages * page // MOBA_BLOCK, rows, lanes), F32),
            pltpu.VMEM((rows, hd), F32),
            pltpu.VMEM((rows, hd), F32),
            pltpu.VMEM((n_slots, pps, page * n_heads, hd), F32),
            pltpu.SemaphoreType.DMA((n_slots,)),
        ],
    )
    return pl.pallas_call(
        functools.partial(_moba_sample_kernel, scale=hd ** -0.5, n_pages=n_pages, n_heads=n_heads),
        grid_spec=grid_spec,
        out_shape=jax.ShapeDtypeStruct((db, rows, hd), F32),
        compiler_params=_params(2),
        name="moba_sample",
    )(page_table, q, k_new, v_new, fold, cache_k, cache_v)


def _fold_matrix(n_blocks, n_heads, lanes):
    r = jnp.arange(lanes)[:, None]
    c = jnp.arange(lanes)[None, :]
    return ((r // n_heads == c) & (r < n_blocks * n_heads)).astype(BF16)


def _trunk(x, mods, mods_kv, hist, pos0, wts, n_heads, hd, bb, tt, attend, is_prompt):
    t = x.shape[1]
    n_a = wts["w_in_pool"].shape[0]
    n_b = wts["w_in_attn"].shape[0]
    cos, sin = _rope_tables(t, hd, pos0)
    new_hist = []
    for l in range(n_a):
        x, hn = _pool_layer(x, mods[l], hist[l], wts["g_pre"][l], wts["g_post"][l], wts["w_in_pool"][l],
                            wts["w_pool_group"][l], wts["pool_scale"][l], wts["w_out_pool"][l], pos0, bb, tt)
        new_hist.append(hn)
    kv = _kv_proj(x, mods_kv, wts["g_kv"], wts["w_kv"], cos, sin, n_heads, hd, bb, tt, is_prompt)
    for jl in range(n_b):
        l = n_a + jl
        q, gate = _q_proj(x, mods[l], wts["g_pre"][l], wts["w_in_attn"][jl], cos, sin, n_heads, hd, bb, tt,
                          head_split=not is_prompt)
        o = attend(q, kv)
        x = _out_proj(o, gate, x, mods[l], wts["g_post"][l], wts["w_out_attn"][jl], bb, tt,
                      head_split=not is_prompt)
    b = x.shape[0]
    return x, kv[0].reshape(b, t, n_heads, hd), kv[1].reshape(b, t, n_heads, hd), jnp.stack(new_hist)


def kernel(x_prompt, x_sample, cache_k, cache_v, state_pool, page_table, c_prompt, c_sample, w_ada, b_ada,
           g_pre, g_post, w_in_pool, w_pool_group, pool_scale, w_out_pool, g_kv, w_ada_kv, b_ada_kv, w_kv,
           w_in_attn, w_out_attn):
    n_b_p = x_prompt.shape[0]
    page, n_heads, hd = cache_k.shape[1], cache_k.shape[2], cache_k.shape[3]
    past_len = page_table.shape[1] * page
    n_a = w_in_pool.shape[0]
    n_hist = state_pool.shape[2]

    c_all = jnp.concatenate([c_prompt, c_sample], axis=0)
    r = c_all.shape[0]
    r_pad = -(-r // V7X_SUBLANES) * V7X_SUBLANES
    c_all = jnp.pad(c_all, ((0, r_pad - r), (0, 0)))
    mods_all = _ada_mods(c_all, w_ada, b_ada)
    mods_kv_all = _ada_mods(c_all, w_ada_kv[None], b_ada_kv[None])[0]
    mods_p = [mods_all[l, 0:n_b_p][:, None, :] for l in range(mods_all.shape[0])]
    mods_s = [mods_all[l, n_b_p:r][:, None, :] for l in range(mods_all.shape[0])]
    mods_kv_p = mods_kv_all[0:n_b_p][:, None, :]
    mods_kv_s = mods_kv_all[n_b_p:r][:, None, :]

    wts = dict(
        g_pre=g_pre, g_post=g_post, pool_scale=pool_scale, g_kv=g_kv,
        w_in_pool=w_in_pool.astype(BF16), w_pool_group=w_pool_group.astype(BF16),
        w_out_pool=w_out_pool.astype(BF16), w_kv=w_kv.astype(BF16),
        w_in_attn=w_in_attn.astype(BF16), w_out_attn=w_out_attn.astype(BF16))

    def attend_prompt(q, kv):
        _, _, kb, vt, kmean = kv
        kmean = kmean.reshape(kmean.shape[0], -1, kmean.shape[-1])
        return _moba_prompt(q, kb, vt, kmean, n_heads, hd, g_heads=PROMPT_HEADS_PER_STEP)

    fold = _fold_matrix(past_len // MOBA_BLOCK, n_heads, V7X_LANES)

    def attend_sample(q, kv):
        return _moba_sample(q, kv[0], kv[1], cache_k, cache_v, page_table, fold)

    hist0 = jnp.zeros((n_a, n_b_p, n_hist, state_pool.shape[3]), F32)
    y_p, k_p, v_p, pool_p = _trunk(x_prompt, mods_p, mods_kv_p, hist0, 0, wts, n_heads, hd,
                                   bb=1, tt=512, attend=attend_prompt, is_prompt=True)
    y_s, k_s, v_s, pool_s = _trunk(x_sample, mods_s, mods_kv_s, state_pool, past_len, wts, n_heads, hd,
                                   bb=16, tt=x_sample.shape[1], attend=attend_sample, is_prompt=False)
    return (y_p, y_s, k_p, v_p, k_s, v_s, pool_p, pool_s)
```

```python
import functools
import math

import jax
import jax.numpy as jnp
from jax import lax
from jax.experimental import pallas as pl
from jax.experimental.pallas import tpu as pltpu

F32 = jnp.float32
BF16 = jnp.bfloat16
NEG_INF = float("-inf")
LOG2E = math.log2(math.e)

POOL_WINDOWS = (2, 4, 8, 16)
MOBA_BLOCK = 256
MOBA_TOPK = 3
ROPE_THETA = 10000.0
NORM_EPS = 1e-6

V7X_LANES = 128
V7X_SUBLANES = 8
V7X_VMEM_LIMIT_BYTES = 56 * 1024 * 1024

HIST_PAD = 16
PROMPT_HEADS_PER_STEP = 8
PROMPT_BLOCKS_PER_ITER = 2
SAMPLE_PAGES_PER_STEP = 8
SAMPLE_RING_SLOTS = 3


def _params(n_axes):
    return pltpu.CompilerParams(
        dimension_semantics=("arbitrary",) * n_axes, vmem_limit_bytes=V7X_VMEM_LIMIT_BYTES)


def _silu(x):
    return x * (1.0 / (1.0 + jnp.exp(-x)))


def _rms(x, g):
    return x * lax.rsqrt(jnp.mean(x * x, axis=-1, keepdims=True) + NORM_EPS) * g


def _dot(a, b):
    return jnp.dot(a, b, preferred_element_type=F32)


def _bucket(idx, size, n):
    out = jnp.zeros_like(idx)
    for h in range(1, n):
        out = out + jnp.where(idx >= h * size, 1, 0)
    return out


def _dot_nt(a, b):
    return lax.dot_general(a, b, (((1,), (1,)), ((), ())), preferred_element_type=F32)


def _rope_table_kernel(cos_ref, sin_ref, *, pos0):
    t, hd = cos_ref.shape
    half = hd // 2
    lane = lax.broadcasted_iota(jnp.int32, (t, hd), 1)
    row = lax.broadcasted_iota(jnp.int32, (t, hd), 0)
    j = jnp.where(lane < half, lane, lane - half).astype(F32)
    inv = jnp.exp(j * (-2.0 * math.log(ROPE_THETA) / hd))
    ang = (row + pos0).astype(F32) * inv
    cos_ref[...] = jnp.cos(ang)
    s = jnp.sin(ang)
    sin_ref[...] = jnp.where(lane < half, -s, s)


def _rope_tables(t, hd, pos0):
    return pl.pallas_call(
        functools.partial(_rope_table_kernel, pos0=pos0),
        out_shape=(jax.ShapeDtypeStruct((t, hd), F32), jax.ShapeDtypeStruct((t, hd), F32)),
        name="rope_tables",
    )()


def _head_rows(ref, h, n_tok, n_heads):
    return (slice(None),) * (len(ref.shape) - 2) + (pl.ds(h, n_tok, stride=n_heads), slice(None))


def _rope_head(a, h, cos, sin, bb, tt, hd):
    slab = a[:, h * hd:(h + 1) * hd]
    rot = pltpu.roll(slab, hd // 2, 1)
    return slab.reshape(bb, tt, hd) * cos[None] + rot.reshape(bb, tt, hd) * sin[None]


def _ada_kernel(c_ref, w_ref, b_ref, o_ref):
    h = _silu(c_ref[...]).astype(BF16)
    o_ref[...] = _dot(h, w_ref[...].astype(BF16)) + b_ref[...]


def _ada_mods(c, w, b):
    n_l, d, n = w.shape
    r = c.shape[0]
    bn = 1024
    return pl.pallas_call(
        _ada_kernel,
        grid=(n_l, n // bn),
        in_specs=[
            pl.BlockSpec((r, d), lambda l, j: (0, 0)),
            pl.BlockSpec((None, d, bn), lambda l, j: (l, 0, j)),
            pl.BlockSpec((None, 1, bn), lambda l, j: (l, 0, j)),
        ],
        out_specs=pl.BlockSpec((None, r, bn), lambda l, j: (l, 0, j)),
        out_shape=jax.ShapeDtypeStruct((n_l, r, n), F32),
        compiler_params=_params(2),
        name="ada_mods",
    )(c, w, b.reshape(n_l, 1, n))


def _pool_kernel(x_ref, mod_ref, hist_ref, gpre_ref, gpost_ref, win_ref, wgrp_ref, pscale_ref, wout_ref,
                 y_ref, hist_out_ref, z_scr, *, pos0):
    t = pl.program_id(1)
    n_t = pl.num_programs(1)
    bb, tt, d = x_ref.shape
    w = z_scr.shape[-1]
    n_hist = hist_ref.shape[1]
    m = bb * tt
    grp = w // len(POOL_WINDOWS)

    @pl.when(t == 0)
    def _():
        z_scr[:, HIST_PAD - n_hist:HIST_PAD, :] = hist_ref[...]

    @pl.when(t > 0)
    def _():
        z_scr[:, 0:HIST_PAD, :] = z_scr[:, tt:tt + HIST_PAD, :]

    mod = mod_ref[...]
    shift, scale, gate = mod[:, :, 0:d], mod[:, :, d:2 * d], mod[:, :, 2 * d:3 * d]
    x = x_ref[...]
    h = _rms(x, gpre_ref[...]) * (1.0 + scale) + shift
    ug = _dot(h.reshape(m, d).astype(BF16), win_ref[...])
    u = ug[:, 0:w]
    z_scr[:, HIST_PAD:HIST_PAD + tt, :] = u.reshape(bb, tt, w)

    pos = pos0 + t * tt + lax.broadcasted_iota(jnp.int32, (1, tt, 1), 1)
    ys = []
    for g, win in enumerate(POOL_WINDOWS):
        sl = slice(g * grp, (g + 1) * grp)
        acc = z_scr[:, HIST_PAD:HIST_PAD + tt, sl]
        for k in range(1, win):
            acc = acc + z_scr[:, HIST_PAD - k:HIST_PAD - k + tt, sl]
        cnt = jnp.minimum(pos + 1, win).astype(F32)
        dm = acc / cnt - z_scr[:, HIST_PAD:HIST_PAD + tt, sl]
        ys.append(_dot(dm.reshape(m, grp).astype(BF16), wgrp_ref[g]))
    y = jnp.concatenate(ys, axis=-1) * pscale_ref[...]
    a = (y * _silu(ug[:, w:2 * w])).astype(BF16)
    o = _dot(a, wout_ref[...]).reshape(bb, tt, d)
    y_ref[...] = x + gate * _rms(o, gpost_ref[...])

    @pl.when(t == n_t - 1)
    def _():
        hist_out_ref[...] = z_scr[:, HIST_PAD + tt - n_hist:HIST_PAD + tt, :]


def _pool_layer(x, mod, hist, g_pre, g_post, w_in, w_grp, pool_scale, w_out, pos0, bb, tt):
    b, t, d = x.shape
    w = w_out.shape[0]
    n_hist = hist.shape[1]
    n_g, grp, _ = w_grp.shape
    const2 = lambda i, j: (0, 0)
    return pl.pallas_call(
        functools.partial(_pool_kernel, pos0=pos0),
        grid=(b // bb, t // tt),
        in_specs=[
            pl.BlockSpec((bb, tt, d), lambda i, j: (i, j, 0)),
            pl.BlockSpec((bb, 1, 3 * d), lambda i, j: (i, 0, 0)),
            pl.BlockSpec((bb, n_hist, w), lambda i, j: (i, 0, 0)),
            pl.BlockSpec((1, d), const2),
            pl.BlockSpec((1, d), const2),
            pl.BlockSpec((d, 2 * w), const2),
            pl.BlockSpec((n_g, grp, grp), lambda i, j: (0, 0, 0)),
            pl.BlockSpec((1, w), const2),
            pl.BlockSpec((w, d), const2),
        ],
        out_specs=(
            pl.BlockSpec((bb, tt, d), lambda i, j: (i, j, 0)),
            pl.BlockSpec((bb, n_hist, w), lambda i, j: (i, 0, 0)),
        ),
        out_shape=(jax.ShapeDtypeStruct((b, t, d), F32), jax.ShapeDtypeStruct((b, n_hist, w), F32)),
        scratch_shapes=[pltpu.VMEM((bb, HIST_PAD + tt, w), F32)],
        compiler_params=_params(2),
        name="pool_layer",
    )(x, mod, hist, g_pre.reshape(1, d), g_post.reshape(1, d), w_in, w_grp, pool_scale.reshape(1, w), w_out)


def _normed(x_ref, mod_ref, g_ref):
    bb, tt, d = x_ref.shape
    mod = mod_ref[...]
    shift, scale = mod[:, :, 0:d], mod[:, :, d:2 * d]
    h = _rms(x_ref[...], g_ref[...]) * (1.0 + scale) + shift
    return h.reshape(bb * tt, d).astype(BF16)


def _kv_kernel(x_ref, mod_ref, g_ref, w_ref, cos_ref, sin_ref, k_ref, v_ref, *extra, n_heads, hd):
    bb, tt, _ = x_ref.shape
    kvw = n_heads * hd
    kv = _dot(_normed(x_ref, mod_ref, g_ref), w_ref[...])
    cos, sin = cos_ref[...], sin_ref[...]
    n_blk = tt // MOBA_BLOCK
    for h in range(n_heads):
        k_h = _rope_head(kv, h, cos, sin, bb, tt, hd)
        v_h = kv[:, kvw + h * hd:kvw + (h + 1) * hd]
        k_ref[_head_rows(k_ref, h, tt, n_heads)] = k_h
        v_ref[_head_rows(v_ref, h, tt, n_heads)] = v_h.reshape(bb, tt, hd)
        if extra:
            kb_ref, vt_ref, kmean_ref = extra
            kb_ref[:, :, h * hd:(h + 1) * hd] = k_h.astype(BF16)
            kmean_ref[:, h * hd:(h + 1) * hd] = jnp.mean(k_h.reshape(n_blk, MOBA_BLOCK, hd), axis=1)
            for j in range(n_blk):
                vt_ref[h, j] = v_h[j * MOBA_BLOCK:(j + 1) * MOBA_BLOCK, :].T.astype(BF16)


def _kv_proj(x, mod, g, w, cos, sin, n_heads, hd, bb, tt, with_attn_layouts):
    b, t, d = x.shape
    kvw = n_heads * hd
    const2 = lambda i, j: (0, 0)
    tile = lambda i, j: (i, j, 0)
    out_specs = [pl.BlockSpec((bb, tt * n_heads, hd), tile), pl.BlockSpec((bb, tt * n_heads, hd), tile)]
    out_shape = [jax.ShapeDtypeStruct((b, t * n_heads, hd), F32), jax.ShapeDtypeStruct((b, t * n_heads, hd), F32)]
    if with_attn_layouts:
        assert bb == 1 and tt % MOBA_BLOCK == 0
        n_blk = tt // MOBA_BLOCK
        out_specs += [
            pl.BlockSpec((bb, tt, kvw), tile),
            pl.BlockSpec((None, n_heads, n_blk, hd, MOBA_BLOCK), lambda i, j: (i, 0, j, 0, 0)),
            pl.BlockSpec((None, None, n_blk, kvw), lambda i, j: (i, j, 0, 0)),
        ]
        out_shape += [
            jax.ShapeDtypeStruct((b, t, kvw), BF16),
            jax.ShapeDtypeStruct((b, n_heads, t // MOBA_BLOCK, hd, MOBA_BLOCK), BF16),
            jax.ShapeDtypeStruct((b, t // tt, n_blk, kvw), F32),
        ]
    return pl.pallas_call(
        functools.partial(_kv_kernel, n_heads=n_heads, hd=hd),
        grid=(b // bb, t // tt),
        in_specs=[
            pl.BlockSpec((bb, tt, d), tile),
            pl.BlockSpec((bb, 1, mod.shape[-1]), lambda i, j: (i, 0, 0)),
            pl.BlockSpec((1, d), const2),
            pl.BlockSpec((d, 2 * kvw), const2),
            pl.BlockSpec((tt, hd), lambda i, j: (j, 0)),
            pl.BlockSpec((tt, hd), lambda i, j: (j, 0)),
        ],
        out_specs=tuple(out_specs),
        out_shape=tuple(out_shape),
        compiler_params=_params(2),
        name="kv_proj",
    )(x, mod, g.reshape(1, d), w, cos, sin)


def _q_kernel(x_ref, mod_ref, g_ref, w_ref, cos_ref, sin_ref, q_ref, gate_ref, *, n_heads, hd, head_split):
    bb, tt, _ = x_ref.shape
    aw = n_heads * hd
    qg = _dot(_normed(x_ref, mod_ref, g_ref), w_ref[...])
    cos, sin = cos_ref[...], sin_ref[...]
    for h in range(n_heads):
        q_h = _rope_head(qg, h, cos, sin, bb, tt, hd)
        if head_split:
            q_ref[_head_rows(q_ref, h, tt, n_heads)] = q_h
        else:
            q_ref[:, :, h * hd:(h + 1) * hd] = q_h.astype(BF16)
    gate_ref[...] = qg[:, aw:2 * aw].reshape(bb, tt, aw).astype(gate_ref.dtype)


def _q_proj(x, mod, g, w, cos, sin, n_heads, hd, bb, tt, head_split):
    b, t, d = x.shape
    aw = n_heads * hd
    const2 = lambda i, j: (0, 0)
    tile = lambda i, j: (i, j, 0)
    if head_split:
        q_spec = pl.BlockSpec((bb, tt * n_heads, hd), tile)
        q_shape = jax.ShapeDtypeStruct((b, t * n_heads, hd), F32)
    else:
        q_spec = pl.BlockSpec((bb, tt, aw), tile)
        q_shape = jax.ShapeDtypeStruct((b, t, aw), BF16)
    return pl.pallas_call(
        functools.partial(_q_kernel, n_heads=n_heads, hd=hd, head_split=head_split),
        grid=(b // bb, t // tt),
        in_specs=[
            pl.BlockSpec((bb, tt, d), tile),
            pl.BlockSpec((bb, 1, mod.shape[-1]), lambda i, j: (i, 0, 0)),
            pl.BlockSpec((1, d), const2),
            pl.BlockSpec((d, 2 * aw), const2),
            pl.BlockSpec((tt, hd), lambda i, j: (j, 0)),
            pl.BlockSpec((tt, hd), lambda i, j: (j, 0)),
        ],
        out_specs=(q_spec, pl.BlockSpec((bb, tt, aw), tile)),
        out_shape=(q_shape, jax.ShapeDtypeStruct((b, t, aw), q_shape.dtype)),
        compiler_params=_params(2),
        name="q_proj",
    )(x, mod, g.reshape(1, d), w, cos, sin)


def _out_kernel(o_ref, gate_ref, x_ref, mod_ref, gpost_ref, w_ref, y_ref, *, head_split):
    bb, tt, d = x_ref.shape
    aw = gate_ref.shape[-1]
    sg = _silu(gate_ref[...].astype(F32))
    if head_split:
        hd = o_ref.shape[2]
        n_heads = aw // hd
        r = jnp.zeros((bb * tt, d), F32)
        for h in range(n_heads):
            o_h = o_ref[_head_rows(o_ref, h, tt, n_heads)]
            a_h = (o_h * sg[:, :, h * hd:(h + 1) * hd]).reshape(bb * tt, hd).astype(BF16)
            r = r + _dot(a_h, w_ref[h * hd:(h + 1) * hd, :])
    else:
        a = (o_ref[...].astype(F32) * sg).reshape(bb * tt, aw).astype(BF16)
        r = _dot(a, w_ref[...])
    res_gate = mod_ref[...][:, :, 2 * d:3 * d]
    y_ref[...] = x_ref[...] + res_gate * _rms(r.reshape(bb, tt, d), gpost_ref[...])


def _out_proj(o, gate, x, mod, g_post, w, bb, tt, head_split):
    b, t, d = x.shape
    aw = gate.shape[-1]
    tile = lambda i, j: (i, j, 0)
    if head_split:
        o_spec = pl.BlockSpec((bb, o.shape[1] // t * tt, o.shape[2]), tile)
    else:
        o_spec = pl.BlockSpec((bb, tt, aw), tile)
    return pl.pallas_call(
        functools.partial(_out_kernel, head_split=head_split),
        grid=(b // bb, t // tt),
        in_specs=[
            o_spec,
            pl.BlockSpec((bb, tt, aw), tile),
            pl.BlockSpec((bb, tt, d), tile),
            pl.BlockSpec((bb, 1, mod.shape[-1]), lambda i, j: (i, 0, 0)),
            pl.BlockSpec((1, d), lambda i, j: (0, 0)),
            pl.BlockSpec((aw, d), lambda i, j: (0, 0)),
        ],
        out_specs=pl.BlockSpec((bb, tt, d), tile),
        out_shape=jax.ShapeDtypeStruct((b, t, d), F32),
        compiler_params=_params(2),
        name="out_proj",
    )(o, gate, x, mod, g_post.reshape(1, d), w)


def _select_blocks_t(gate_t, n_past):
    nb = gate_t.shape[0]
    blk = lax.broadcasted_iota(jnp.int32, gate_t.shape, 0)
    rank = jnp.zeros(gate_t.shape, jnp.int32)
    for m in range(nb):
        gm = gate_t[m:m + 1, :]
        ahead = jnp.where(gm > gate_t, 1, jnp.where(gm == gate_t, jnp.where(m < blk, 1, 0), 0))
        rank = rank + jnp.where(m < n_past, ahead, 0)
    return jnp.where(blk < n_past, jnp.where(rank < MOBA_TOPK, 1, 0), 0)


def _moba_prompt_kernel(q_ref, k_ref, vt_ref, kmean_ref, o_ref, shift_scr, s_scr, *, scale, hd):
    i = pl.program_id(2)
    g_heads = q_ref.shape[1] // hd
    c = scale * LOG2E
    start = pl.multiple_of(i * MOBA_BLOCK, MOBA_BLOCK)
    key_i = lax.broadcasted_iota(jnp.int32, (MOBA_BLOCK, MOBA_BLOCK), 0)
    qry_i = lax.broadcasted_iota(jnp.int32, (MOBA_BLOCK, MOBA_BLOCK), 1)
    causal = key_i <= qry_i

    q_ts, s_owns, m0s = [], [], []
    for g in range(g_heads):
        ln = slice(g * hd, (g + 1) * hd)
        q_t = q_ref[:, ln].astype(F32).T.astype(BF16)
        gate_t = _dot(kmean_ref[:, ln].astype(BF16), q_t)
        keep = _select_blocks_t(gate_t, i)
        shift_scr[g] = jnp.where(keep > 0, 0.0, NEG_INF)
        s = jnp.where(causal, _dot(k_ref[pl.ds(start, MOBA_BLOCK), ln], q_t), NEG_INF)
        q_ts.append(q_t)
        s_owns.append(s)
        m0s.append(jnp.max(s, axis=0, keepdims=True))

    per_iter = PROMPT_BLOCKS_PER_ITER
    n_full = i // per_iter
    n_rem = i - n_full * per_iter

    def over_past_blocks(blocks_fn, carry):
        carry = lax.fori_loop(0, n_full, lambda t, cr: blocks_fn(t * per_iter, per_iter, cr), carry)
        for r in range(1, per_iter):
            carry = lax.cond(n_rem >= r, lambda cr, r=r: blocks_fn(n_full * per_iter + (r - 1), 1, cr),
                             lambda cr: cr, carry)
        return carry

    def scores_blocks(n0, count, ms):
        out = list(ms)
        for u in range(count):
            n = n0 + u
            st = pl.multiple_of(n * MOBA_BLOCK, MOBA_BLOCK)
            for g in range(g_heads):
                sn = _dot(k_ref[pl.ds(st, MOBA_BLOCK), g * hd:(g + 1) * hd], q_ts[g])
                s_scr[g, n] = sn
                out[g] = jnp.maximum(out[g], jnp.max(sn, axis=0, keepdims=True) + shift_scr[g, pl.ds(n, 1), :])
        return tuple(out)

    ms = over_past_blocks(scores_blocks, tuple(m0s))

    carry0 = []
    for g in range(g_heads):
        p = jnp.exp2((s_owns[g] - ms[g]) * c)
        carry0 += [jnp.sum(p, axis=0, keepdims=True), _dot(vt_ref[g, i], p.astype(BF16))]

    def values_blocks(n0, count, carry):
        out = list(carry)
        for u in range(count):
            n = n0 + u
            for g in range(g_heads):
                pn = jnp.exp2((s_scr[g, n] - (ms[g] - shift_scr[g, pl.ds(n, 1), :])) * c)
                out[2 * g] = out[2 * g] + jnp.sum(pn, axis=0, keepdims=True)
                out[2 * g + 1] = out[2 * g + 1] + _dot(vt_ref[g, n], pn.astype(BF16))
        return tuple(out)

    fin = over_past_blocks(values_blocks, tuple(carry0))
    for g in range(g_heads):
        o_ref[:, g * hd:(g + 1) * hd] = (fin[2 * g + 1] / fin[2 * g]).T.astype(o_ref.dtype)


def _moba_prompt(q, kb, vt, kmean, n_heads, hd, g_heads):
    b, s, aw = q.shape
    n_blk = s // MOBA_BLOCK
    gw = g_heads * hd
    return pl.pallas_call(
        functools.partial(_moba_prompt_kernel, scale=hd ** -0.5, hd=hd),
        grid=(b, n_heads // g_heads, n_blk),
        in_specs=[
            pl.BlockSpec((None, MOBA_BLOCK, gw), lambda bi, h, i: (bi, i, h)),
            pl.BlockSpec((None, s, gw), lambda bi, h, i: (bi, 0, h), pipeline_mode=pl.Buffered(1)),
            pl.BlockSpec((None, g_heads, n_blk, hd, MOBA_BLOCK), lambda bi, h, i: (bi, h, 0, 0, 0),
                         pipeline_mode=pl.Buffered(1)),
            pl.BlockSpec((None, n_blk, gw), lambda bi, h, i: (bi, 0, h)),
        ],
        out_specs=pl.BlockSpec((None, MOBA_BLOCK, gw), lambda bi, h, i: (bi, i, h)),
        out_shape=jax.ShapeDtypeStruct((b, s, aw), BF16),
        scratch_shapes=[
            pltpu.VMEM((g_heads, n_blk, MOBA_BLOCK), F32),
            pltpu.VMEM((g_heads, n_blk, MOBA_BLOCK, MOBA_BLOCK), F32),
        ],
        compiler_params=_params(3),
        name="moba_prompt",
    )(q, kb, vt, kmean)


def _moba_sample_kernel(pt_ref, q_ref, kn_ref, vn_ref, expand_ref, ck_ref, cv_ref, o_ref,
                        kmean_scr, s_scr, bmax_scr, shift_scr, l_scr, acc_scr, ring, sems,
                        *, scale, n_pages, n_heads):
    seq = pl.program_id(0)
    j = pl.program_id(1)
    n_slots, pps, pr, hd = ring.shape
    ks = n_pages // pps
    steps = 2 * ks
    look = n_slots - 1
    rows = q_ref.shape[0]
    t_new = rows // n_heads
    page = pr // n_heads

    def page_copy(cache_ref, page_id, slot, a):
        return pltpu.make_async_copy(cache_ref.at[page_id], ring.at[slot, a], sems.at[slot])

    def start_step(seq_i, j_i, slot):
        is_k = j_i < ks
        first = jnp.where(is_k, j_i, j_i - ks) * pps
        for a in range(pps):
            page_id = pt_ref[seq_i, first + a]

            @pl.when(is_k)
            def _():
                page_copy(ck_ref, page_id, slot, a).start()

            @pl.when(jnp.logical_not(is_k))
            def _():
                page_copy(cv_ref, page_id, slot, a).start()

    @pl.when(jnp.logical_and(seq == 0, j == 0))
    def _():
        for ahead in range(look):
            start_step(0, ahead, ahead)

    g = seq * steps + j
    j_next = j + look
    wrap = j_next >= steps
    seq_next = jnp.where(wrap, seq + 1, seq)

    @pl.when(seq_next < pl.num_programs(0))
    def _():
        start_step(seq_next, jnp.where(wrap, j_next - steps, j_next), lax.rem(g + look, n_slots))

    slot = lax.rem(g, n_slots)
    for a in range(pps):
        page_copy(ck_ref, 0, slot, a).wait()
    ppb = MOBA_BLOCK // page
    n_blocks = n_pages // ppb
    bps = pps // ppb
    lanes = kmean_scr.shape[0]
    c = scale * LOG2E
    q_all = q_ref[...].astype(BF16)

    r_i = lax.broadcasted_iota(jnp.int32, (rows, 1), 0)
    c_i = lax.broadcasted_iota(jnp.int32, (1, lanes), 1)
    r_tok = _bucket(r_i, n_heads, t_new)
    c_grp = _bucket(c_i, n_heads, lanes // n_heads)
    same_head = (r_i - n_heads * r_tok) == (c_i - n_heads * c_grp)

    def page_wide(tile):
        return jnp.concatenate([tile] * (pr // lanes), axis=1)

    @pl.when(j == 0)
    def _():
        kmean_scr[...] = jnp.zeros_like(kmean_scr)

    @pl.when(j < ks)
    def _():
        head_bias = page_wide(jnp.where(same_head, 0.0, NEG_INF))
        sums, maxes = [], []
        for a in range(pps):
            kp = ring[slot, a]
            sums.append(jnp.sum(kp.reshape(page, n_heads, hd), axis=0))
            s = _dot_nt(q_all, kp.astype(BF16))
            s_scr[j * pps + a] = s
            maxes.append(jnp.max(s + head_bias, axis=-1, keepdims=True))
        for b in range(bps):
            blk_sum, blk_max = sums[b * ppb], maxes[b * ppb]
            for e in range(1, ppb):
                blk_sum = blk_sum + sums[b * ppb + e]
                blk_max = jnp.maximum(blk_max, maxes[b * ppb + e])
            r0 = pl.multiple_of((j * bps + b) * n_heads, n_heads)
            kmean_scr[pl.ds(r0, n_heads), :] = blk_sum * (1.0 / MOBA_BLOCK)
            bmax_scr[j * bps + b] = jnp.broadcast_to(blk_max, (rows, lanes))

    @pl.when(j == ks - 1)
    def _():
        q_pad = jnp.concatenate([q_all, jnp.zeros((lanes - rows, hd), BF16)], axis=0)
        gate_t = _dot_nt(kmean_scr[...].astype(BF16), q_pad)
        tiles = [gate_t[n * n_heads:(n + 1) * n_heads, :] for n in range(n_blocks)]
        sub_i = lax.broadcasted_iota(jnp.int32, (n_heads, lanes), 0)
        own = jnp.where(c_i < rows, jnp.where(sub_i == c_i - n_heads * c_grp, 1.0, 0.0), 0.0)
        keep_tiles = []
        for n in range(n_blocks):
            rank = jnp.zeros((n_heads, lanes), jnp.int32)
            for m in range(n_blocks):
                if m < n:
                    rank = rank + jnp.where(tiles[m] >= tiles[n], 1, 0)
                elif m > n:
                    rank = rank + jnp.where(tiles[m] > tiles[n], 1, 0)
            keep_tiles.append(jnp.where(rank < MOBA_TOPK, own, 0.0))
        keep_tiles.append(jnp.zeros((lanes - n_blocks * n_heads, lanes), F32))
        keep = jnp.concatenate(keep_tiles, axis=0).T
        keep_x = _dot(keep.astype(BF16), expand_ref[...])[0:rows, :]

        pad = jnp.zeros((lanes - rows, hd), F32)
        kn = jnp.concatenate([kn_ref[...], pad], axis=0).astype(BF16)
        vn = jnp.concatenate([vn_ref[...], pad], axis=0).astype(BF16)
        own_ok = jnp.where(c_i < rows, jnp.where(same_head, jnp.where(c_grp <= r_tok, 1, 0), 0), 0)
        s_own = jnp.where(own_ok > 0, _dot_nt(q_all, kn), NEG_INF)

        kept = [keep_x[:, n * lanes:(n + 1) * lanes] > 0.5 for n in range(n_blocks)]
        m_lane = jnp.full((rows, lanes), NEG_INF, F32)
        for n in range(n_blocks):
            m_lane = jnp.maximum(m_lane, jnp.where(kept[n], bmax_scr[n], NEG_INF))
        m_fin = jnp.maximum(jnp.max(s_own, axis=-1, keepdims=True),
                            jnp.max(m_lane, axis=-1, keepdims=True))
        neg_m = jnp.broadcast_to(-m_fin, (rows, lanes))
        for n in range(n_blocks):
            shift_scr[n] = jnp.where(kept[n], neg_m, NEG_INF)
        p_own = jnp.exp2((s_own - m_fin) * c)
        l_scr[...] = jnp.broadcast_to(jnp.sum(p_own, axis=-1, keepdims=True), l_scr.shape)
        acc_scr[...] = _dot(p_own.astype(BF16), vn)

    @pl.when(j >= ks)
    def _():
        acc = acc_scr[...]
        l_add = jnp.zeros((rows, 1), F32)
        for b in range(bps):
            shift = page_wide(shift_scr[(j - ks) * bps + b])
            for e in range(ppb):
                a = b * ppb + e
                pn = jnp.exp2((s_scr[(j - ks) * pps + a] + shift) * c)
                l_add = l_add + jnp.sum(pn, axis=-1, keepdims=True)
                acc = acc + _dot(pn.astype(BF16), ring[slot, a].astype(BF16))
        acc_scr[...] = acc
        l_scr[...] = l_scr[...] + l_add

    @pl.when(j == 2 * ks - 1)
    def _():
        o_ref[...] = acc_scr[...] / l_scr[...]


def _moba_sample(q, k_new, v_new, cache_k, cache_v, page_table, expand):
    n_phys, page, n_heads, hd = cache_k.shape
    db, t_new = q.shape[0], q.shape[1] // n_heads
    cache_k = cache_k.reshape(n_phys, page * n_heads, hd)
    cache_v = cache_v.reshape(n_phys, page * n_heads, hd)
    n_pages = page_table.shape[1]
    pps = SAMPLE_PAGES_PER_STEP
    n_slots = SAMPLE_RING_SLOTS
    assert MOBA_BLOCK % page == 0 and pps % (MOBA_BLOCK // page) == 0 and n_pages % pps == 0
    ks = n_pages // pps
    assert n_slots - 1 <= 2 * ks
    rows = t_new * n_heads
    lanes = V7X_LANES
    assert rows <= lanes and (n_pages * page // MOBA_BLOCK) * n_heads <= lanes and lanes % n_heads == 0
    assert (page * n_heads) % lanes == 0

    seq = lambda s, j, pt: (s, 0, 0)
    new_spec = pl.BlockSpec((None, rows, hd), seq)
    grid_spec = pltpu.PrefetchScalarGridSpec(
        num_scalar_prefetch=1,
        grid=(db, 2 * ks),
        in_specs=[new_spec, new_spec, new_spec, pl.BlockSpec(expand.shape, lambda s, j, pt: (0, 0)),
                  pl.BlockSpec(memory_space=pl.ANY), pl.BlockSpec(memory_space=pl.ANY)],
        out_specs=new_spec,
        scratch_shapes=[
            pltpu.VMEM((lanes, hd), F32),
            pltpu.VMEM((n_pages, rows, page * n_heads), F32),
            pltpu.VMEM((n_pages * page // MOBA_BLOCK, rows, lanes), F32),
            pltpu.VMEM((n_pages * page // MOBA_BLOCK, rows, lanes), F32),
            pltpu.VMEM((rows, hd), F32),
            pltpu.VMEM((rows, hd), F32),
            pltpu.VMEM((n_slots, pps, page * n_heads, hd), F32),
            pltpu.SemaphoreType.DMA((n_slots,)),
        ],
    )
    return pl.pallas_call(
        functools.partial(_moba_sample_kernel, scale=hd ** -0.5, n_pages=n_pages, n_heads=n_heads),
        grid_spec=grid_spec,
        out_shape=jax.ShapeDtypeStruct((db, rows, hd), F32),
        compiler_params=_params(2),
        name="moba_sample",
    )(page_table, q, k_new, v_new, expand, cache_k, cache_v)


def _expand_matrix(n_blocks, n_heads, lanes):
    r = jnp.arange(lanes)[:, None]
    c = jnp.arange(n_blocks * lanes)[None, :]
    return ((r // n_heads == c // lanes) & (r % n_heads == c % n_heads) & (r < n_blocks * n_heads)).astype(BF16)


def _trunk(x, mods, mods_kv, hist, pos0, wts, n_heads, hd, bb, tt, attend, is_prompt):
    t = x.shape[1]
    n_a = wts["w_in_pool"].shape[0]
    n_b = wts["w_in_attn"].shape[0]
    cos, sin = _rope_tables(t, hd, pos0)
    new_hist = []
    for l in range(n_a):
        x, hn = _pool_layer(x, mods[l], hist[l], wts["g_pre"][l], wts["g_post"][l], wts["w_in_pool"][l],
                            wts["w_pool_group"][l], wts["pool_scale"][l], wts["w_out_pool"][l], pos0, bb, tt)
        new_hist.append(hn)
    kv = _kv_proj(x, mods_kv, wts["g_kv"], wts["w_kv"], cos, sin, n_heads, hd, bb, tt, is_prompt)
    for jl in range(n_b):
        l = n_a + jl
        q, gate = _q_proj(x, mods[l], wts["g_pre"][l], wts["w_in_attn"][jl], cos, sin, n_heads, hd, bb, tt,
                          head_split=not is_prompt)
        o = attend(q, kv)
        x = _out_proj(o, gate, x, mods[l], wts["g_post"][l], wts["w_out_attn"][jl], bb, tt,
                      head_split=not is_prompt)
    b = x.shape[0]
    return x, kv[0].reshape(b, t, n_heads, hd), kv[1].reshape(b, t, n_heads, hd), jnp.stack(new_hist)


def kernel(x_prompt, x_sample, cache_k, cache_v, state_pool, page_table, c_prompt, c_sample, w_ada, b_ada,
           g_pre, g_post, w_in_pool, w_pool_group, pool_scale, w_out_pool, g_kv, w_ada_kv, b_ada_kv, w_kv,
           w_in_attn, w_out_attn):
    n_b_p = x_prompt.shape[0]
    page, n_heads, hd = cache_k.shape[1], cache_k.shape[2], cache_k.shape[3]
    past_len = page_table.shape[1] * page
    n_a = w_in_pool.shape[0]
    n_hist = state_pool.shape[2]

    c_all = jnp.concatenate([c_prompt, c_sample], axis=0)
    r = c_all.shape[0]
    r_pad = -(-r // V7X_SUBLANES) * V7X_SUBLANES
    c_all = jnp.pad(c_all, ((0, r_pad - r), (0, 0)))
    mods_all = _ada_mods(c_all, w_ada, b_ada)
    mods_kv_all = _ada_mods(c_all, w_ada_kv[None], b_ada_kv[None])[0]
    mods_p = [mods_all[l, 0:n_b_p][:, None, :] for l in range(mods_all.shape[0])]
    mods_s = [mods_all[l, n_b_p:r][:, None, :] for l in range(mods_all.shape[0])]
    mods_kv_p = mods_kv_all[0:n_b_p][:, None, :]
    mods_kv_s = mods_kv_all[n_b_p:r][:, None, :]

    wts = dict(
        g_pre=g_pre, g_post=g_post, pool_scale=pool_scale, g_kv=g_kv,
        w_in_pool=w_in_pool.astype(BF16), w_pool_group=w_pool_group.astype(BF16),
        w_out_pool=w_out_pool.astype(BF16), w_kv=w_kv.astype(BF16),
        w_in_attn=w_in_attn.astype(BF16), w_out_attn=w_out_attn.astype(BF16))

    def attend_prompt(q, kv):
        _, _, kb, vt, kmean = kv
        kmean = kmean.reshape(kmean.shape[0], -1, kmean.shape[-1])
        return _moba_prompt(q, kb, vt, kmean, n_heads, hd, g_heads=PROMPT_HEADS_PER_STEP)

    expand = _expand_matrix(past_len // MOBA_BLOCK, n_heads, V7X_LANES)

    def attend_sample(q, kv):
        return _moba_sample(q, kv[0], kv[1], cache_k, cache_v, page_table, expand)

    hist0 = jnp.zeros((n_a, n_b_p, n_hist, state_pool.shape[3]), F32)
    y_p, k_p, v_p, pool_p = _trunk(x_prompt, mods_p, mods_kv_p, hist0, 0, wts, n_heads, hd,
                                   bb=1, tt=512, attend=attend_prompt, is_prompt=True)
    y_s, k_s, v_s, pool_s = _trunk(x_sample, mods_s, mods_kv_s, state_pool, past_len, wts, n_heads, hd,
                                   bb=16, tt=x_sample.shape[1], attend=attend_sample, is_prompt=False)
    return (y_p, y_s, k_p, v_p, k_s, v_s, pool_p, pool_s)
```

```python
import functools
import math

import jax
import jax.numpy as jnp
from jax import lax
from jax.experimental import pallas as pl
from jax.experimental.pallas import tpu as pltpu

F32 = jnp.float32
BF16 = jnp.bfloat16
NEG_INF = float("-inf")
LOG2E = math.log2(math.e)

POOL_WINDOWS = (2, 4, 8, 16)
MOBA_BLOCK = 256
MOBA_TOPK = 3
ROPE_THETA = 10000.0
NORM_EPS = 1e-6

V7X_LANES = 128
V7X_SUBLANES = 8
V7X_VMEM_LIMIT_BYTES = 56 * 1024 * 1024

HIST_PAD = 16
SUM_ROWS = 16
PROMPT_HEADS_PER_STEP = 8
PROMPT_BLOCKS_PER_ITER = 2
SAMPLE_PAGES_PER_STEP = 8
SAMPLE_RING_SLOTS = 5


def _params(n_axes):
    return pltpu.CompilerParams(
        dimension_semantics=("arbitrary",) * n_axes, vmem_limit_bytes=V7X_VMEM_LIMIT_BYTES)


def _silu(x):
    return x * (1.0 / (1.0 + jnp.exp(-x)))


def _rms(x, g):
    return x * lax.rsqrt(jnp.mean(x * x, axis=-1, keepdims=True) + NORM_EPS) * g


def _dot(a, b):
    return jnp.dot(a, b, preferred_element_type=F32)


def _bucket(idx, size, n):
    out = jnp.zeros_like(idx)
    for h in range(1, n):
        out = out + jnp.where(idx >= h * size, 1, 0)
    return out


def _dot_nt(a, b):
    return lax.dot_general(a, b, (((1,), (1,)), ((), ())), preferred_element_type=F32)


def _rope_table_kernel(cos_ref, sin_ref, *, pos0):
    t, hd = cos_ref.shape
    half = hd // 2
    lane = lax.broadcasted_iota(jnp.int32, (t, hd), 1)
    row = lax.broadcasted_iota(jnp.int32, (t, hd), 0)
    j = jnp.where(lane < half, lane, lane - half).astype(F32)
    inv = jnp.exp(j * (-2.0 * math.log(ROPE_THETA) / hd))
    ang = (row + pos0).astype(F32) * inv
    cos_ref[...] = jnp.cos(ang)
    s = jnp.sin(ang)
    sin_ref[...] = jnp.where(lane < half, -s, s)


def _rope_tables(t, hd, pos0):
    return pl.pallas_call(
        functools.partial(_rope_table_kernel, pos0=pos0),
        out_shape=(jax.ShapeDtypeStruct((t, hd), F32), jax.ShapeDtypeStruct((t, hd), F32)),
        name="rope_tables",
    )()


def _head_rows(ref, h, n_tok, n_heads):
    return (slice(None),) * (len(ref.shape) - 2) + (pl.ds(h, n_tok, stride=n_heads), slice(None))


def _rope_head(a, h, cos, sin, bb, tt, hd):
    slab = a[:, h * hd:(h + 1) * hd]
    rot = pltpu.roll(slab, hd // 2, 1)
    return slab.reshape(bb, tt, hd) * cos[None] + rot.reshape(bb, tt, hd) * sin[None]


def _ada_kernel(c_ref, w_ref, b_ref, o_ref):
    h = _silu(c_ref[...]).astype(BF16)
    o_ref[...] = _dot(h, w_ref[...].astype(BF16)) + b_ref[...]


def _ada_mods(c, w, b):
    n_l, d, n = w.shape
    r = c.shape[0]
    bn = 1024
    return pl.pallas_call(
        _ada_kernel,
        grid=(n_l, n // bn),
        in_specs=[
            pl.BlockSpec((r, d), lambda l, j: (0, 0)),
            pl.BlockSpec((None, d, bn), lambda l, j: (l, 0, j)),
            pl.BlockSpec((None, 1, bn), lambda l, j: (l, 0, j)),
        ],
        out_specs=pl.BlockSpec((None, r, bn), lambda l, j: (l, 0, j)),
        out_shape=jax.ShapeDtypeStruct((n_l, r, n), F32),
        compiler_params=_params(2),
        name="ada_mods",
    )(c, w, b.reshape(n_l, 1, n))


def _pool_kernel(x_ref, mod_ref, hist_ref, gpre_ref, gpost_ref, win_ref, wgrp_ref, pscale_ref, wout_ref,
                 y_ref, hist_out_ref, z_scr, *, pos0):
    t = pl.program_id(1)
    n_t = pl.num_programs(1)
    bb, tt, d = x_ref.shape
    w = z_scr.shape[-1]
    n_hist = hist_ref.shape[1]
    m = bb * tt
    grp = w // len(POOL_WINDOWS)

    @pl.when(t == 0)
    def _():
        z_scr[:, HIST_PAD - n_hist:HIST_PAD, :] = hist_ref[...]

    @pl.when(t > 0)
    def _():
        z_scr[:, 0:HIST_PAD, :] = z_scr[:, tt:tt + HIST_PAD, :]

    mod = mod_ref[...]
    shift, scale, gate = mod[:, :, 0:d], mod[:, :, d:2 * d], mod[:, :, 2 * d:3 * d]
    x = x_ref[...]
    h = _rms(x, gpre_ref[...]) * (1.0 + scale) + shift
    ug = _dot(h.reshape(m, d).astype(BF16), win_ref[...])
    u = ug[:, 0:w]
    z_scr[:, HIST_PAD:HIST_PAD + tt, :] = u.reshape(bb, tt, w)

    pos = pos0 + t * tt + lax.broadcasted_iota(jnp.int32, (1, tt, 1), 1)
    ys = []
    for g, win in enumerate(POOL_WINDOWS):
        sl = slice(g * grp, (g + 1) * grp)
        acc = z_scr[:, HIST_PAD:HIST_PAD + tt, sl]
        for k in range(1, win):
            acc = acc + z_scr[:, HIST_PAD - k:HIST_PAD - k + tt, sl]
        cnt = jnp.minimum(pos + 1, win).astype(F32)
        dm = acc / cnt - z_scr[:, HIST_PAD:HIST_PAD + tt, sl]
        ys.append(_dot(dm.reshape(m, grp).astype(BF16), wgrp_ref[g]))
    y = jnp.concatenate(ys, axis=-1) * pscale_ref[...]
    a = (y * _silu(ug[:, w:2 * w])).astype(BF16)
    o = _dot(a, wout_ref[...]).reshape(bb, tt, d)
    y_ref[...] = x + gate * _rms(o, gpost_ref[...])

    @pl.when(t == n_t - 1)
    def _():
        hist_out_ref[...] = z_scr[:, HIST_PAD + tt - n_hist:HIST_PAD + tt, :]


def _pool_layer(x, mod, hist, g_pre, g_post, w_in, w_grp, pool_scale, w_out, pos0, bb, tt):
    b, t, d = x.shape
    w = w_out.shape[0]
    n_hist = hist.shape[1]
    n_g, grp, _ = w_grp.shape
    const2 = lambda i, j: (0, 0)
    return pl.pallas_call(
        functools.partial(_pool_kernel, pos0=pos0),
        grid=(b // bb, t // tt),
        in_specs=[
            pl.BlockSpec((bb, tt, d), lambda i, j: (i, j, 0)),
            pl.BlockSpec((bb, 1, 3 * d), lambda i, j: (i, 0, 0)),
            pl.BlockSpec((bb, n_hist, w), lambda i, j: (i, 0, 0)),
            pl.BlockSpec((1, d), const2),
            pl.BlockSpec((1, d), const2),
            pl.BlockSpec((d, 2 * w), const2),
            pl.BlockSpec((n_g, grp, grp), lambda i, j: (0, 0, 0)),
            pl.BlockSpec((1, w), const2),
            pl.BlockSpec((w, d), const2),
        ],
        out_specs=(
            pl.BlockSpec((bb, tt, d), lambda i, j: (i, j, 0)),
            pl.BlockSpec((bb, n_hist, w), lambda i, j: (i, 0, 0)),
        ),
        out_shape=(jax.ShapeDtypeStruct((b, t, d), F32), jax.ShapeDtypeStruct((b, n_hist, w), F32)),
        scratch_shapes=[pltpu.VMEM((bb, HIST_PAD + tt, w), F32)],
        compiler_params=_params(2),
        name="pool_layer",
    )(x, mod, hist, g_pre.reshape(1, d), g_post.reshape(1, d), w_in, w_grp, pool_scale.reshape(1, w), w_out)


def _normed(x_ref, mod_ref, g_ref):
    bb, tt, d = x_ref.shape
    mod = mod_ref[...]
    shift, scale = mod[:, :, 0:d], mod[:, :, d:2 * d]
    h = _rms(x_ref[...], g_ref[...]) * (1.0 + scale) + shift
    return h.reshape(bb * tt, d).astype(BF16)


def _kv_kernel(x_ref, mod_ref, g_ref, w_ref, cos_ref, sin_ref, k_ref, v_ref, *extra, n_heads, hd):
    bb, tt, _ = x_ref.shape
    kvw = n_heads * hd
    kv = _dot(_normed(x_ref, mod_ref, g_ref), w_ref[...])
    cos, sin = cos_ref[...], sin_ref[...]
    n_blk = tt // MOBA_BLOCK
    for h in range(n_heads):
        k_h = _rope_head(kv, h, cos, sin, bb, tt, hd)
        v_h = kv[:, kvw + h * hd:kvw + (h + 1) * hd]
        k_ref[_head_rows(k_ref, h, tt, n_heads)] = k_h
        v_ref[_head_rows(v_ref, h, tt, n_heads)] = v_h.reshape(bb, tt, hd)
        if extra:
            kb_ref, vt_ref, kmean_ref = extra
            kb_ref[:, :, h * hd:(h + 1) * hd] = k_h.astype(BF16)
            kmean_ref[:, h * hd:(h + 1) * hd] = jnp.mean(k_h.reshape(n_blk, MOBA_BLOCK, hd), axis=1)
            for j in range(n_blk):
                vt_ref[h, j] = v_h[j * MOBA_BLOCK:(j + 1) * MOBA_BLOCK, :].T.astype(BF16)


def _kv_proj(x, mod, g, w, cos, sin, n_heads, hd, bb, tt, with_attn_layouts):
    b, t, d = x.shape
    kvw = n_heads * hd
    const2 = lambda i, j: (0, 0)
    tile = lambda i, j: (i, j, 0)
    out_specs = [pl.BlockSpec((bb, tt * n_heads, hd), tile), pl.BlockSpec((bb, tt * n_heads, hd), tile)]
    out_shape = [jax.ShapeDtypeStruct((b, t * n_heads, hd), F32), jax.ShapeDtypeStruct((b, t * n_heads, hd), F32)]
    if with_attn_layouts:
        assert bb == 1 and tt % MOBA_BLOCK == 0
        n_blk = tt // MOBA_BLOCK
        out_specs += [
            pl.BlockSpec((bb, tt, kvw), tile),
            pl.BlockSpec((None, n_heads, n_blk, hd, MOBA_BLOCK), lambda i, j: (i, 0, j, 0, 0)),
            pl.BlockSpec((None, None, n_blk, kvw), lambda i, j: (i, j, 0, 0)),
        ]
        out_shape += [
            jax.ShapeDtypeStruct((b, t, kvw), BF16),
            jax.ShapeDtypeStruct((b, n_heads, t // MOBA_BLOCK, hd, MOBA_BLOCK), BF16),
            jax.ShapeDtypeStruct((b, t // tt, n_blk, kvw), F32),
        ]
    return pl.pallas_call(
        functools.partial(_kv_kernel, n_heads=n_heads, hd=hd),
        grid=(b // bb, t // tt),
        in_specs=[
            pl.BlockSpec((bb, tt, d), tile),
            pl.BlockSpec((bb, 1, mod.shape[-1]), lambda i, j: (i, 0, 0)),
            pl.BlockSpec((1, d), const2),
            pl.BlockSpec((d, 2 * kvw), const2),
            pl.BlockSpec((tt, hd), lambda i, j: (j, 0)),
            pl.BlockSpec((tt, hd), lambda i, j: (j, 0)),
        ],
        out_specs=tuple(out_specs),
        out_shape=tuple(out_shape),
        compiler_params=_params(2),
        name="kv_proj",
    )(x, mod, g.reshape(1, d), w, cos, sin)


def _q_kernel(x_ref, mod_ref, g_ref, w_ref, cos_ref, sin_ref, q_ref, gate_ref, *, n_heads, hd, head_split):
    bb, tt, _ = x_ref.shape
    aw = n_heads * hd
    qg = _dot(_normed(x_ref, mod_ref, g_ref), w_ref[...])
    cos, sin = cos_ref[...], sin_ref[...]
    for h in range(n_heads):
        q_h = _rope_head(qg, h, cos, sin, bb, tt, hd)
        if head_split:
            q_ref[_head_rows(q_ref, h, tt, n_heads)] = q_h
        else:
            q_ref[:, :, h * hd:(h + 1) * hd] = q_h.astype(BF16)
    gate_ref[...] = qg[:, aw:2 * aw].reshape(bb, tt, aw).astype(gate_ref.dtype)


def _q_proj(x, mod, g, w, cos, sin, n_heads, hd, bb, tt, head_split):
    b, t, d = x.shape
    aw = n_heads * hd
    const2 = lambda i, j: (0, 0)
    tile = lambda i, j: (i, j, 0)
    if head_split:
        q_spec = pl.BlockSpec((bb, tt * n_heads, hd), tile)
        q_shape = jax.ShapeDtypeStruct((b, t * n_heads, hd), F32)
    else:
        q_spec = pl.BlockSpec((bb, tt, aw), tile)
        q_shape = jax.ShapeDtypeStruct((b, t, aw), BF16)
    return pl.pallas_call(
        functools.partial(_q_kernel, n_heads=n_heads, hd=hd, head_split=head_split),
        grid=(b // bb, t // tt),
        in_specs=[
            pl.BlockSpec((bb, tt, d), tile),
            pl.BlockSpec((bb, 1, mod.shape[-1]), lambda i, j: (i, 0, 0)),
            pl.BlockSpec((1, d), const2),
            pl.BlockSpec((d, 2 * aw), const2),
            pl.BlockSpec((tt, hd), lambda i, j: (j, 0)),
            pl.BlockSpec((tt, hd), lambda i, j: (j, 0)),
        ],
        out_specs=(q_spec, pl.BlockSpec((bb, tt, aw), tile)),
        out_shape=(q_shape, jax.ShapeDtypeStruct((b, t, aw), q_shape.dtype)),
        compiler_params=_params(2),
        name="q_proj",
    )(x, mod, g.reshape(1, d), w, cos, sin)


def _out_kernel(o_ref, gate_ref, x_ref, mod_ref, gpost_ref, w_ref, y_ref, *, head_split):
    bb, tt, d = x_ref.shape
    aw = gate_ref.shape[-1]
    sg = _silu(gate_ref[...].astype(F32))
    if head_split:
        hd = o_ref.shape[2]
        n_heads = aw // hd
        r = jnp.zeros((bb * tt, d), F32)
        for h in range(n_heads):
            o_h = o_ref[_head_rows(o_ref, h, tt, n_heads)]
            a_h = (o_h * sg[:, :, h * hd:(h + 1) * hd]).reshape(bb * tt, hd).astype(BF16)
            r = r + _dot(a_h, w_ref[h * hd:(h + 1) * hd, :])
    else:
        a = (o_ref[...].astype(F32) * sg).reshape(bb * tt, aw).astype(BF16)
        r = _dot(a, w_ref[...])
    res_gate = mod_ref[...][:, :, 2 * d:3 * d]
    y_ref[...] = x_ref[...] + res_gate * _rms(r.reshape(bb, tt, d), gpost_ref[...])


def _out_proj(o, gate, x, mod, g_post, w, bb, tt, head_split):
    b, t, d = x.shape
    aw = gate.shape[-1]
    tile = lambda i, j: (i, j, 0)
    if head_split:
        o_spec = pl.BlockSpec((bb, o.shape[1] // t * tt, o.shape[2]), tile)
    else:
        o_spec = pl.BlockSpec((bb, tt, aw), tile)
    return pl.pallas_call(
        functools.partial(_out_kernel, head_split=head_split),
        grid=(b // bb, t // tt),
        in_specs=[
            o_spec,
            pl.BlockSpec((bb, tt, aw), tile),
            pl.BlockSpec((bb, tt, d), tile),
            pl.BlockSpec((bb, 1, mod.shape[-1]), lambda i, j: (i, 0, 0)),
            pl.BlockSpec((1, d), lambda i, j: (0, 0)),
            pl.BlockSpec((aw, d), lambda i, j: (0, 0)),
        ],
        out_specs=pl.BlockSpec((bb, tt, d), tile),
        out_shape=jax.ShapeDtypeStruct((b, t, d), F32),
        compiler_params=_params(2),
        name="out_proj",
    )(o, gate, x, mod, g_post.reshape(1, d), w)


def _select_blocks_t(gate_t, n_past):
    nb = gate_t.shape[0]
    blk = lax.broadcasted_iota(jnp.int32, gate_t.shape, 0)
    rank = jnp.zeros(gate_t.shape, jnp.int32)
    for m in range(nb):
        gm = gate_t[m:m + 1, :]
        ahead = jnp.where(gm > gate_t, 1, jnp.where(gm == gate_t, jnp.where(m < blk, 1, 0), 0))
        rank = rank + jnp.where(m < n_past, ahead, 0)
    return jnp.where(blk < n_past, jnp.where(rank < MOBA_TOPK, 1, 0), 0)


def _moba_prompt_kernel(q_ref, k_ref, vt_ref, kmean_ref, o_ref, shift_scr, s_scr, *, scale, hd):
    i = pl.program_id(2)
    g_heads = q_ref.shape[1] // hd
    c = scale * LOG2E
    ones_rows = jnp.ones((SUM_ROWS, MOBA_BLOCK), BF16)

    def values_and_sums(g, n, p):
        return _dot(jnp.concatenate([vt_ref[g, n], ones_rows], axis=0), p.astype(BF16))

    start = pl.multiple_of(i * MOBA_BLOCK, MOBA_BLOCK)
    key_i = lax.broadcasted_iota(jnp.int32, (MOBA_BLOCK, MOBA_BLOCK), 0)
    qry_i = lax.broadcasted_iota(jnp.int32, (MOBA_BLOCK, MOBA_BLOCK), 1)
    causal = key_i <= qry_i

    q_ts, s_owns, m0s = [], [], []
    for g in range(g_heads):
        ln = slice(g * hd, (g + 1) * hd)
        q_t = q_ref[:, ln].astype(F32).T.astype(BF16)
        gate_t = _dot(kmean_ref[:, ln].astype(BF16), q_t)
        keep = _select_blocks_t(gate_t, i)
        shift_scr[g] = jnp.where(keep > 0, 0.0, NEG_INF)
        s = jnp.where(causal, _dot(k_ref[pl.ds(start, MOBA_BLOCK), ln], q_t) * c, NEG_INF)
        q_ts.append(q_t)
        s_owns.append(s)
        m0s.append(jnp.max(s, axis=0, keepdims=True))

    per_iter = PROMPT_BLOCKS_PER_ITER
    n_full = i // per_iter
    n_rem = i - n_full * per_iter

    def over_past_blocks(blocks_fn, carry):
        carry = lax.fori_loop(0, n_full, lambda t, cr: blocks_fn(t * per_iter, per_iter, cr), carry)
        for r in range(1, per_iter):
            carry = lax.cond(n_rem >= r, lambda cr, r=r: blocks_fn(n_full * per_iter + (r - 1), 1, cr),
                             lambda cr: cr, carry)
        return carry

    def scores_blocks(n0, count, ms):
        out = list(ms)
        for u in range(count):
            n = n0 + u
            st = pl.multiple_of(n * MOBA_BLOCK, MOBA_BLOCK)
            for g in range(g_heads):
                sn = _dot(k_ref[pl.ds(st, MOBA_BLOCK), g * hd:(g + 1) * hd], q_ts[g]) * c
                s_scr[g, n] = sn
                out[g] = jnp.maximum(out[g], jnp.max(sn, axis=0, keepdims=True) + shift_scr[g, pl.ds(n, 1), :])
        return tuple(out)

    ms = over_past_blocks(scores_blocks, tuple(m0s))

    accs0 = tuple(values_and_sums(g, i, jnp.exp2(s_owns[g] - ms[g])) for g in range(g_heads))

    def values_blocks(n0, count, accs):
        out = list(accs)
        for u in range(count):
            n = n0 + u
            for g in range(g_heads):
                pn = jnp.exp2(s_scr[g, n] - (ms[g] - shift_scr[g, pl.ds(n, 1), :]))
                out[g] = out[g] + values_and_sums(g, n, pn)
        return tuple(out)

    accs = over_past_blocks(values_blocks, accs0)
    for g in range(g_heads):
        o_ref[:, g * hd:(g + 1) * hd] = (accs[g][0:hd, :] / accs[g][hd:hd + 1, :]).T.astype(o_ref.dtype)


def _moba_prompt(q, kb, vt, kmean, n_heads, hd, g_heads):
    b, s, aw = q.shape
    n_blk = s // MOBA_BLOCK
    gw = g_heads * hd
    return pl.pallas_call(
        functools.partial(_moba_prompt_kernel, scale=hd ** -0.5, hd=hd),
        grid=(b, n_heads // g_heads, n_blk),
        in_specs=[
            pl.BlockSpec((None, MOBA_BLOCK, gw), lambda bi, h, i: (bi, i, h)),
            pl.BlockSpec((None, s, gw), lambda bi, h, i: (bi, 0, h), pipeline_mode=pl.Buffered(1)),
            pl.BlockSpec((None, g_heads, n_blk, hd, MOBA_BLOCK), lambda bi, h, i: (bi, h, 0, 0, 0),
                         pipeline_mode=pl.Buffered(1)),
            pl.BlockSpec((None, n_blk, gw), lambda bi, h, i: (bi, 0, h)),
        ],
        out_specs=pl.BlockSpec((None, MOBA_BLOCK, gw), lambda bi, h, i: (bi, i, h)),
        out_shape=jax.ShapeDtypeStruct((b, s, aw), BF16),
        scratch_shapes=[
            pltpu.VMEM((g_heads, n_blk, MOBA_BLOCK), F32),
            pltpu.VMEM((g_heads, n_blk, MOBA_BLOCK, MOBA_BLOCK), F32),
        ],
        compiler_params=_params(3),
        name="moba_prompt",
    )(q, kb, vt, kmean)


def _moba_sample_kernel(pt_ref, q_ref, kn_ref, vn_ref, expand_ref, ck_ref, cv_ref, o_ref,
                        kmean_scr, s_scr, bmax_scr, shift_scr, l_scr, acc_scr, ring, sems,
                        *, scale, n_pages, n_heads):
    seq = pl.program_id(0)
    j = pl.program_id(1)
    n_slots, pps, pr, hd = ring.shape
    ks = n_pages // pps
    steps = 2 * ks
    look = n_slots - 1
    rows = q_ref.shape[0]
    t_new = rows // n_heads
    page = pr // n_heads

    def page_copy(cache_ref, page_id, slot, a):
        return pltpu.make_async_copy(cache_ref.at[page_id], ring.at[slot, a], sems.at[slot])

    def start_step(seq_i, j_i, slot):
        is_k = j_i < ks
        first = jnp.where(is_k, j_i, j_i - ks) * pps
        for a in range(pps):
            page_id = pt_ref[seq_i, first + a]

            @pl.when(is_k)
            def _():
                page_copy(ck_ref, page_id, slot, a).start()

            @pl.when(jnp.logical_not(is_k))
            def _():
                page_copy(cv_ref, page_id, slot, a).start()

    @pl.when(jnp.logical_and(seq == 0, j == 0))
    def _():
        for ahead in range(look):
            start_step(0, ahead, ahead)

    g = seq * steps + j
    j_next = j + look
    wrap = j_next >= steps
    seq_next = jnp.where(wrap, seq + 1, seq)

    @pl.when(seq_next < pl.num_programs(0))
    def _():
        start_step(seq_next, jnp.where(wrap, j_next - steps, j_next), lax.rem(g + look, n_slots))

    slot = lax.rem(g, n_slots)
    for a in range(pps):
        page_copy(ck_ref, 0, slot, a).wait()
    ppb = MOBA_BLOCK // page
    n_blocks = n_pages // ppb
    bps = pps // ppb
    lanes = kmean_scr.shape[0]
    c = scale * LOG2E
    q_all = q_ref[...].astype(BF16)

    r_i = lax.broadcasted_iota(jnp.int32, (rows, 1), 0)
    c_i = lax.broadcasted_iota(jnp.int32, (1, lanes), 1)
    r_tok = _bucket(r_i, n_heads, t_new)
    c_grp = _bucket(c_i, n_heads, lanes // n_heads)
    same_head = (r_i - n_heads * r_tok) == (c_i - n_heads * c_grp)

    def page_wide(tile):
        return jnp.concatenate([tile] * (pr // lanes), axis=1)

    @pl.when(j == 0)
    def _():
        kmean_scr[...] = jnp.zeros_like(kmean_scr)

    @pl.when(j < ks)
    def _():
        head_bias = page_wide(jnp.where(same_head, 0.0, NEG_INF))
        sums, maxes = [], []
        for a in range(pps):
            kp = ring[slot, a]
            sums.append(jnp.sum(kp.reshape(page, n_heads, hd), axis=0))
            s = _dot_nt(q_all, kp.astype(BF16))
            s_scr[j * pps + a] = s
            maxes.append(jnp.max(s + head_bias, axis=-1, keepdims=True))
        for b in range(bps):
            blk_sum, blk_max = sums[b * ppb], maxes[b * ppb]
            for e in range(1, ppb):
                blk_sum = blk_sum + sums[b * ppb + e]
                blk_max = jnp.maximum(blk_max, maxes[b * ppb + e])
            r0 = pl.multiple_of((j * bps + b) * n_heads, n_heads)
            kmean_scr[pl.ds(r0, n_heads), :] = blk_sum * (1.0 / MOBA_BLOCK)
            bmax_scr[j * bps + b] = jnp.broadcast_to(blk_max, (rows, lanes))

    @pl.when(j == ks - 1)
    def _():
        q_pad = jnp.concatenate([q_all, jnp.zeros((lanes - rows, hd), BF16)], axis=0)
        gate_t = _dot_nt(kmean_scr[...].astype(BF16), q_pad)
        tiles = [gate_t[n * n_heads:(n + 1) * n_heads, :] for n in range(n_blocks)]
        sub_i = lax.broadcasted_iota(jnp.int32, (n_heads, lanes), 0)
        own = jnp.where(c_i < rows, jnp.where(sub_i == c_i - n_heads * c_grp, 1.0, 0.0), 0.0)
        keep_tiles = []
        for n in range(n_blocks):
            rank = jnp.zeros((n_heads, lanes), jnp.int32)
            for m in range(n_blocks):
                if m < n:
                    rank = rank + jnp.where(tiles[m] >= tiles[n], 1, 0)
                elif m > n:
                    rank = rank + jnp.where(tiles[m] > tiles[n], 1, 0)
            keep_tiles.append(jnp.where(rank < MOBA_TOPK, own, 0.0))
        keep_tiles.append(jnp.zeros((lanes - n_blocks * n_heads, lanes), F32))
        keep = jnp.concatenate(keep_tiles, axis=0).T
        keep_x = _dot(keep.astype(BF16), expand_ref[...])[0:rows, :]

        pad = jnp.zeros((lanes - rows, hd), F32)
        kn = jnp.concatenate([kn_ref[...], pad], axis=0).astype(BF16)
        vn = jnp.concatenate([vn_ref[...], pad], axis=0).astype(BF16)
        own_ok = jnp.where(c_i < rows, jnp.where(same_head, jnp.where(c_grp <= r_tok, 1, 0), 0), 0)
        s_own = jnp.where(own_ok > 0, _dot_nt(q_all, kn), NEG_INF)

        kept = [keep_x[:, n * lanes:(n + 1) * lanes] > 0.5 for n in range(n_blocks)]
        m_lane = jnp.full((rows, lanes), NEG_INF, F32)
        for n in range(n_blocks):
            m_lane = jnp.maximum(m_lane, jnp.where(kept[n], bmax_scr[n], NEG_INF))
        m_fin = jnp.maximum(jnp.max(s_own, axis=-1, keepdims=True),
                            jnp.max(m_lane, axis=-1, keepdims=True))
        neg_m = jnp.broadcast_to(-m_fin, (rows, lanes))
        for n in range(n_blocks):
            shift_scr[n] = jnp.where(kept[n], neg_m, NEG_INF)
        p_own = jnp.exp2((s_own - m_fin) * c)
        l_scr[...] = jnp.broadcast_to(jnp.sum(p_own, axis=-1, keepdims=True), l_scr.shape)
        acc_scr[...] = _dot(p_own.astype(BF16), vn)

    @pl.when(j >= ks)
    def _():
        acc = acc_scr[...]
        l_add = jnp.zeros((rows, 1), F32)
        for b in range(bps):
            shift = page_wide(shift_scr[(j - ks) * bps + b])
            for e in range(ppb):
                a = b * ppb + e
                pn = jnp.exp2((s_scr[(j - ks) * pps + a] + shift) * c)
                l_add = l_add + jnp.sum(pn, axis=-1, keepdims=True)
                acc = acc + _dot(pn.astype(BF16), ring[slot, a].astype(BF16))
        acc_scr[...] = acc
        l_scr[...] = l_scr[...] + l_add

    @pl.when(j == 2 * ks - 1)
    def _():
        o_ref[...] = acc_scr[...] / l_scr[...]


def _moba_sample(q, k_new, v_new, cache_k, cache_v, page_table, expand):
    n_phys, page, n_heads, hd = cache_k.shape
    db, t_new = q.shape[0], q.shape[1] // n_heads
    cache_k = cache_k.reshape(n_phys, page * n_heads, hd)
    cache_v = cache_v.reshape(n_phys, page * n_heads, hd)
    n_pages = page_table.shape[1]
    pps = SAMPLE_PAGES_PER_STEP
    n_slots = SAMPLE_RING_SLOTS
    assert MOBA_BLOCK % page == 0 and pps % (MOBA_BLOCK // page) == 0 and n_pages % pps == 0
    ks = n_pages // pps
    assert n_slots - 1 <= 2 * ks
    rows = t_new * n_heads
    lanes = V7X_LANES
    assert rows <= lanes and (n_pages * page // MOBA_BLOCK) * n_heads <= lanes and lanes % n_heads == 0
    assert (page * n_heads) % lanes == 0

    seq = lambda s, j, pt: (s, 0, 0)
    new_spec = pl.BlockSpec((None, rows, hd), seq)
    grid_spec = pltpu.PrefetchScalarGridSpec(
        num_scalar_prefetch=1,
        grid=(db, 2 * ks),
        in_specs=[new_spec, new_spec, new_spec, pl.BlockSpec(expand.shape, lambda s, j, pt: (0, 0)),
                  pl.BlockSpec(memory_space=pl.ANY), pl.BlockSpec(memory_space=pl.ANY)],
        out_specs=new_spec,
        scratch_shapes=[
            pltpu.VMEM((lanes, hd), F32),
            pltpu.VMEM((n_pages, rows, page * n_heads), F32),
            pltpu.VMEM((n_pages * page // MOBA_BLOCK, rows, lanes), F32),
            pltpu.VMEM((n_pages * page // MOBA_BLOCK, rows, lanes), F32),
            pltpu.VMEM((rows, hd), F32),
            pltpu.VMEM((rows, hd), F32),
            pltpu.VMEM((n_slots, pps, page * n_heads, hd), F32),
            pltpu.SemaphoreType.DMA((n_slots,)),
        ],
    )
    return pl.pallas_call(
        functools.partial(_moba_sample_kernel, scale=hd ** -0.5, n_pages=n_pages, n_heads=n_heads),
        grid_spec=grid_spec,
        out_shape=jax.ShapeDtypeStruct((db, rows, hd), F32),
        compiler_params=_params(2),
        name="moba_sample",
    )(page_table, q, k_new, v_new, expand, cache_k, cache_v)


def _expand_matrix(n_blocks, n_heads, lanes):
    r = jnp.arange(lanes)[:, None]
    c = jnp.arange(n_blocks * lanes)[None, :]
    return ((r // n_heads == c // lanes) & (r % n_heads == c % n_heads) & (r < n_blocks * n_heads)).astype(BF16)


def _trunk(x, mods, mods_kv, hist, pos0, wts, n_heads, hd, bb, tt, attend, is_prompt):
    t = x.shape[1]
    n_a = wts["w_in_pool"].shape[0]
    n_b = wts["w_in_attn"].shape[0]
    cos, sin = _rope_tables(t, hd, pos0)
    new_hist = []
    for l in range(n_a):
        x, hn = _pool_layer(x, mods[l], hist[l], wts["g_pre"][l], wts["g_post"][l], wts["w_in_pool"][l],
                            wts["w_pool_group"][l], wts["pool_scale"][l], wts["w_out_pool"][l], pos0, bb, tt)
        new_hist.append(hn)
    kv = _kv_proj(x, mods_kv, wts["g_kv"], wts["w_kv"], cos, sin, n_heads, hd, bb, tt, is_prompt)
    for jl in range(n_b):
        l = n_a + jl
        q, gate = _q_proj(x, mods[l], wts["g_pre"][l], wts["w_in_attn"][jl], cos, sin, n_heads, hd, bb, tt,
                          head_split=not is_prompt)
        o = attend(q, kv)
        x = _out_proj(o, gate, x, mods[l], wts["g_post"][l], wts["w_out_attn"][jl], bb, tt,
                      head_split=not is_prompt)
    b = x.shape[0]
    return x, kv[0].reshape(b, t, n_heads, hd), kv[1].reshape(b, t, n_heads, hd), jnp.stack(new_hist)


def kernel(x_prompt, x_sample, cache_k, cache_v, state_pool, page_table, c_prompt, c_sample, w_ada, b_ada,
           g_pre, g_post, w_in_pool, w_pool_group, pool_scale, w_out_pool, g_kv, w_ada_kv, b_ada_kv, w_kv,
           w_in_attn, w_out_attn):
    n_b_p = x_prompt.shape[0]
    page, n_heads, hd = cache_k.shape[1], cache_k.shape[2], cache_k.shape[3]
    past_len = page_table.shape[1] * page
    n_a = w_in_pool.shape[0]
    n_hist = state_pool.shape[2]

    c_all = jnp.concatenate([c_prompt, c_sample], axis=0)
    r = c_all.shape[0]
    r_pad = -(-r // V7X_SUBLANES) * V7X_SUBLANES
    c_all = jnp.pad(c_all, ((0, r_pad - r), (0, 0)))
    mods_all = _ada_mods(c_all, w_ada, b_ada)
    mods_kv_all = _ada_mods(c_all, w_ada_kv[None], b_ada_kv[None])[0]
    mods_p = [mods_all[l, 0:n_b_p][:, None, :] for l in range(mods_all.shape[0])]
    mods_s = [mods_all[l, n_b_p:r][:, None, :] for l in range(mods_all.shape[0])]
    mods_kv_p = mods_kv_all[0:n_b_p][:, None, :]
    mods_kv_s = mods_kv_all[n_b_p:r][:, None, :]

    wts = dict(
        g_pre=g_pre, g_post=g_post, pool_scale=pool_scale, g_kv=g_kv,
        w_in_pool=w_in_pool.astype(BF16), w_pool_group=w_pool_group.astype(BF16),
        w_out_pool=w_out_pool.astype(BF16), w_kv=w_kv.astype(BF16),
        w_in_attn=w_in_attn.astype(BF16), w_out_attn=w_out_attn.astype(BF16))

    def attend_prompt(q, kv):
        _, _, kb, vt, kmean = kv
        kmean = kmean.reshape(kmean.shape[0], -1, kmean.shape[-1])
        return _moba_prompt(q, kb, vt, kmean, n_heads, hd, g_heads=PROMPT_HEADS_PER_STEP)

    expand = _expand_matrix(past_len // MOBA_BLOCK, n_heads, V7X_LANES)

    def attend_sample(q, kv):
        return _moba_sample(q, kv[0], kv[1], cache_k, cache_v, page_table, expand)

    hist0 = jnp.zeros((n_a, n_b_p, n_hist, state_pool.shape[3]), F32)
    y_p, k_p, v_p, pool_p = _trunk(x_prompt, mods_p, mods_kv_p, hist0, 0, wts, n_heads, hd,
                                   bb=1, tt=512, attend=attend_prompt, is_prompt=True)
    y_s, k_s, v_s, pool_s = _trunk(x_sample, mods_s, mods_kv_s, state_pool, past_len, wts, n_heads, hd,
                                   bb=16, tt=x_sample.shape[1], attend=attend_sample, is_prompt=False)
    return (y_p, y_s, k_p, v_p, k_s, v_s, pool_p, pool_s)
```

```python
import functools
import math

import jax
import jax.numpy as jnp
from jax import lax
from jax.experimental import pallas as pl
from jax.experimental.pallas import tpu as pltpu

F32 = jnp.float32
BF16 = jnp.bfloat16
NEG_INF = float("-inf")
LOG2E = math.log2(math.e)

POOL_WINDOWS = (2, 4, 8, 16)
MOBA_BLOCK = 256
MOBA_TOPK = 3
ROPE_THETA = 10000.0
NORM_EPS = 1e-6

V7X_LANES = 128
V7X_SUBLANES = 8
V7X_VMEM_LIMIT_BYTES = 56 * 1024 * 1024

HIST_PAD = 16
SUM_ROWS = 16
PROMPT_HEADS_PER_STEP = 8
PROMPT_BLOCKS_PER_ITER = 2
SAMPLE_PAGES_PER_STEP = 16
SAMPLE_RING_SLOTS = 3
PROMPT_TOKENS_PER_STEP = 512
SAMPLE_SEQS_PER_STEP = 64


def _params(n_axes):
    return pltpu.CompilerParams(
        dimension_semantics=("arbitrary",) * n_axes, vmem_limit_bytes=V7X_VMEM_LIMIT_BYTES)


def _silu(x):
    return x * (1.0 / (1.0 + jnp.exp(-x)))


def _rms(x, g):
    return x * lax.rsqrt(jnp.mean(x * x, axis=-1, keepdims=True) + NORM_EPS) * g


def _dot(a, b):
    return jnp.dot(a, b, preferred_element_type=F32)


def _bucket(idx, size, n):
    out = jnp.zeros_like(idx)
    for h in range(1, n):
        out = out + jnp.where(idx >= h * size, 1, 0)
    return out


def _dot_nt(a, b):
    return lax.dot_general(a, b, (((1,), (1,)), ((), ())), preferred_element_type=F32)


def _rope_table_kernel(cos_ref, sin_ref, *, pos0):
    t, hd = cos_ref.shape
    half = hd // 2
    lane = lax.broadcasted_iota(jnp.int32, (t, hd), 1)
    row = lax.broadcasted_iota(jnp.int32, (t, hd), 0)
    j = jnp.where(lane < half, lane, lane - half).astype(F32)
    inv = jnp.exp(j * (-2.0 * math.log(ROPE_THETA) / hd))
    ang = (row + pos0).astype(F32) * inv
    cos_ref[...] = jnp.cos(ang)
    s = jnp.sin(ang)
    sin_ref[...] = jnp.where(lane < half, -s, s)


def _rope_tables(t, hd, pos0):
    return pl.pallas_call(
        functools.partial(_rope_table_kernel, pos0=pos0),
        out_shape=(jax.ShapeDtypeStruct((t, hd), F32), jax.ShapeDtypeStruct((t, hd), F32)),
        name="rope_tables",
    )()


def _head_rows(ref, h, n_tok, n_heads):
    return (slice(None),) * (len(ref.shape) - 2) + (pl.ds(h, n_tok, stride=n_heads), slice(None))


def _rope_head(a, h, cos, sin, bb, tt, hd):
    slab = a[:, h * hd:(h + 1) * hd]
    rot = pltpu.roll(slab, hd // 2, 1)
    return slab.reshape(bb, tt, hd) * cos[None] + rot.reshape(bb, tt, hd) * sin[None]


def _ada_kernel(c_ref, w_ref, b_ref, o_ref):
    h = _silu(c_ref[...]).astype(BF16)
    o_ref[...] = _dot(h, w_ref[...].astype(BF16)) + b_ref[...]


def _ada_mods(c, w, b):
    n_l, d, n = w.shape
    r = c.shape[0]
    bn = 1024
    return pl.pallas_call(
        _ada_kernel,
        grid=(n_l, n // bn),
        in_specs=[
            pl.BlockSpec((r, d), lambda l, j: (0, 0)),
            pl.BlockSpec((None, d, bn), lambda l, j: (l, 0, j)),
            pl.BlockSpec((None, 1, bn), lambda l, j: (l, 0, j)),
        ],
        out_specs=pl.BlockSpec((None, r, bn), lambda l, j: (l, 0, j)),
        out_shape=jax.ShapeDtypeStruct((n_l, r, n), F32),
        compiler_params=_params(2),
        name="ada_mods",
    )(c, w, b.reshape(n_l, 1, n))


def _pool_kernel(x_ref, mod_ref, hist_ref, gpre_ref, gpost_ref, win_ref, wgrp_ref, pscale_ref, wout_ref,
                 y_ref, hist_out_ref, z_scr, *, pos0):
    t = pl.program_id(1)
    n_t = pl.num_programs(1)
    bb, tt, d = x_ref.shape
    w = z_scr.shape[-1]
    n_hist = hist_ref.shape[1]
    m = bb * tt
    grp = w // len(POOL_WINDOWS)

    @pl.when(t == 0)
    def _():
        z_scr[:, HIST_PAD - n_hist:HIST_PAD, :] = hist_ref[...]

    @pl.when(t > 0)
    def _():
        z_scr[:, 0:HIST_PAD, :] = z_scr[:, tt:tt + HIST_PAD, :]

    mod = mod_ref[...]
    shift, scale, gate = mod[:, :, 0:d], mod[:, :, d:2 * d], mod[:, :, 2 * d:3 * d]
    x = x_ref[...]
    h = _rms(x, gpre_ref[...]) * (1.0 + scale) + shift
    ug = _dot(h.reshape(m, d).astype(BF16), win_ref[...])
    u = ug[:, 0:w]
    z_scr[:, HIST_PAD:HIST_PAD + tt, :] = u.reshape(bb, tt, w)

    pos = pos0 + t * tt + lax.broadcasted_iota(jnp.int32, (1, tt, 1), 1)
    ys = []
    for g, win in enumerate(POOL_WINDOWS):
        sl = slice(g * grp, (g + 1) * grp)
        acc = z_scr[:, HIST_PAD:HIST_PAD + tt, sl]
        for k in range(1, win):
            acc = acc + z_scr[:, HIST_PAD - k:HIST_PAD - k + tt, sl]
        cnt = jnp.minimum(pos + 1, win).astype(F32)
        dm = acc / cnt - z_scr[:, HIST_PAD:HIST_PAD + tt, sl]
        ys.append(_dot(dm.reshape(m, grp).astype(BF16), wgrp_ref[g]))
    y = jnp.concatenate(ys, axis=-1) * pscale_ref[...]
    a = (y * _silu(ug[:, w:2 * w])).astype(BF16)
    o = _dot(a, wout_ref[...]).reshape(bb, tt, d)
    y_ref[...] = x + gate * _rms(o, gpost_ref[...])

    @pl.when(t == n_t - 1)
    def _():
        hist_out_ref[...] = z_scr[:, HIST_PAD + tt - n_hist:HIST_PAD + tt, :]


def _pool_layer(x, mod, hist, g_pre, g_post, w_in, w_grp, pool_scale, w_out, pos0, bb, tt):
    b, t, d = x.shape
    w = w_out.shape[0]
    n_hist = hist.shape[1]
    n_g, grp, _ = w_grp.shape
    const2 = lambda i, j: (0, 0)
    return pl.pallas_call(
        functools.partial(_pool_kernel, pos0=pos0),
        grid=(b // bb, t // tt),
        in_specs=[
            pl.BlockSpec((bb, tt, d), lambda i, j: (i, j, 0)),
            pl.BlockSpec((bb, 1, 3 * d), lambda i, j: (i, 0, 0)),
            pl.BlockSpec((bb, n_hist, w), lambda i, j: (i, 0, 0)),
            pl.BlockSpec((1, d), const2),
            pl.BlockSpec((1, d), const2),
            pl.BlockSpec((d, 2 * w), const2),
            pl.BlockSpec((n_g, grp, grp), lambda i, j: (0, 0, 0)),
            pl.BlockSpec((1, w), const2),
            pl.BlockSpec((w, d), const2),
        ],
        out_specs=(
            pl.BlockSpec((bb, tt, d), lambda i, j: (i, j, 0)),
            pl.BlockSpec((bb, n_hist, w), lambda i, j: (i, 0, 0)),
        ),
        out_shape=(jax.ShapeDtypeStruct((b, t, d), F32), jax.ShapeDtypeStruct((b, n_hist, w), F32)),
        scratch_shapes=[pltpu.VMEM((bb, HIST_PAD + tt, w), F32)],
        compiler_params=_params(2),
        name="pool_layer",
    )(x, mod, hist, g_pre.reshape(1, d), g_post.reshape(1, d), w_in, w_grp, pool_scale.reshape(1, w), w_out)


def _normed(x_ref, mod_ref, g_ref):
    bb, tt, d = x_ref.shape
    mod = mod_ref[...]
    shift, scale = mod[:, :, 0:d], mod[:, :, d:2 * d]
    h = _rms(x_ref[...], g_ref[...]) * (1.0 + scale) + shift
    return h.reshape(bb * tt, d).astype(BF16)


def _kv_kernel(x_ref, mod_ref, g_ref, w_ref, cos_ref, sin_ref, k_ref, v_ref, *extra, n_heads, hd):
    bb, tt, _ = x_ref.shape
    kvw = n_heads * hd
    kv = _dot(_normed(x_ref, mod_ref, g_ref), w_ref[...])
    cos, sin = cos_ref[...], sin_ref[...]
    n_blk = tt // MOBA_BLOCK
    for h in range(n_heads):
        k_h = _rope_head(kv, h, cos, sin, bb, tt, hd)
        v_h = kv[:, kvw + h * hd:kvw + (h + 1) * hd]
        k_ref[_head_rows(k_ref, h, tt, n_heads)] = k_h
        v_ref[_head_rows(v_ref, h, tt, n_heads)] = v_h.reshape(bb, tt, hd)
        if extra:
            kb_ref, vt_ref, kmean_ref = extra
            kb_ref[:, :, h * hd:(h + 1) * hd] = k_h.astype(BF16)
            kmean_ref[:, h * hd:(h + 1) * hd] = jnp.mean(k_h.reshape(n_blk, MOBA_BLOCK, hd), axis=1)
            for j in range(n_blk):
                vt_ref[h, j] = v_h[j * MOBA_BLOCK:(j + 1) * MOBA_BLOCK, :].T.astype(BF16)


def _kv_proj(x, mod, g, w, cos, sin, n_heads, hd, bb, tt, with_attn_layouts):
    b, t, d = x.shape
    kvw = n_heads * hd
    const2 = lambda i, j: (0, 0)
    tile = lambda i, j: (i, j, 0)
    out_specs = [pl.BlockSpec((bb, tt * n_heads, hd), tile), pl.BlockSpec((bb, tt * n_heads, hd), tile)]
    out_shape = [jax.ShapeDtypeStruct((b, t * n_heads, hd), F32), jax.ShapeDtypeStruct((b, t * n_heads, hd), F32)]
    if with_attn_layouts:
        assert bb == 1 and tt % MOBA_BLOCK == 0
        n_blk = tt // MOBA_BLOCK
        out_specs += [
            pl.BlockSpec((bb, tt, kvw), tile),
            pl.BlockSpec((None, n_heads, n_blk, hd, MOBA_BLOCK), lambda i, j: (i, 0, j, 0, 0)),
            pl.BlockSpec((None, None, n_blk, kvw), lambda i, j: (i, j, 0, 0)),
        ]
        out_shape += [
            jax.ShapeDtypeStruct((b, t, kvw), BF16),
            jax.ShapeDtypeStruct((b, n_heads, t // MOBA_BLOCK, hd, MOBA_BLOCK), BF16),
            jax.ShapeDtypeStruct((b, t // tt, n_blk, kvw), F32),
        ]
    return pl.pallas_call(
        functools.partial(_kv_kernel, n_heads=n_heads, hd=hd),
        grid=(b // bb, t // tt),
        in_specs=[
            pl.BlockSpec((bb, tt, d), tile),
            pl.BlockSpec((bb, 1, mod.shape[-1]), lambda i, j: (i, 0, 0)),
            pl.BlockSpec((1, d), const2),
            pl.BlockSpec((d, 2 * kvw), const2),
            pl.BlockSpec((tt, hd), lambda i, j: (j, 0)),
            pl.BlockSpec((tt, hd), lambda i, j: (j, 0)),
        ],
        out_specs=tuple(out_specs),
        out_shape=tuple(out_shape),
        compiler_params=_params(2),
        name="kv_proj",
    )(x, mod, g.reshape(1, d), w, cos, sin)


def _q_kernel(x_ref, mod_ref, g_ref, w_ref, cos_ref, sin_ref, q_ref, gate_ref, *, n_heads, hd, head_split):
    bb, tt, _ = x_ref.shape
    aw = n_heads * hd
    qg = _dot(_normed(x_ref, mod_ref, g_ref), w_ref[...])
    cos, sin = cos_ref[...], sin_ref[...]
    for h in range(n_heads):
        q_h = _rope_head(qg, h, cos, sin, bb, tt, hd)
        if head_split:
            q_ref[_head_rows(q_ref, h, tt, n_heads)] = q_h
        else:
            q_ref[:, :, h * hd:(h + 1) * hd] = q_h.astype(BF16)
    gate_ref[...] = qg[:, aw:2 * aw].reshape(bb, tt, aw).astype(gate_ref.dtype)


def _q_proj(x, mod, g, w, cos, sin, n_heads, hd, bb, tt, head_split):
    b, t, d = x.shape
    aw = n_heads * hd
    const2 = lambda i, j: (0, 0)
    tile = lambda i, j: (i, j, 0)
    if head_split:
        q_spec = pl.BlockSpec((bb, tt * n_heads, hd), tile)
        q_shape = jax.ShapeDtypeStruct((b, t * n_heads, hd), F32)
    else:
        q_spec = pl.BlockSpec((bb, tt, aw), tile)
        q_shape = jax.ShapeDtypeStruct((b, t, aw), BF16)
    return pl.pallas_call(
        functools.partial(_q_kernel, n_heads=n_heads, hd=hd, head_split=head_split),
        grid=(b // bb, t // tt),
        in_specs=[
            pl.BlockSpec((bb, tt, d), tile),
            pl.BlockSpec((bb, 1, mod.shape[-1]), lambda i, j: (i, 0, 0)),
            pl.BlockSpec((1, d), const2),
            pl.BlockSpec((d, 2 * aw), const2),
            pl.BlockSpec((tt, hd), lambda i, j: (j, 0)),
            pl.BlockSpec((tt, hd), lambda i, j: (j, 0)),
        ],
        out_specs=(q_spec, pl.BlockSpec((bb, tt, aw), tile)),
        out_shape=(q_shape, jax.ShapeDtypeStruct((b, t, aw), q_shape.dtype)),
        compiler_params=_params(2),
        name="q_proj",
    )(x, mod, g.reshape(1, d), w, cos, sin)


def _out_kernel(o_ref, gate_ref, x_ref, mod_ref, gpost_ref, w_ref, y_ref, *, head_split):
    bb, tt, d = x_ref.shape
    aw = gate_ref.shape[-1]
    sg = _silu(gate_ref[...].astype(F32))
    if head_split:
        hd = o_ref.shape[2]
        n_heads = aw // hd
        r = jnp.zeros((bb * tt, d), F32)
        for h in range(n_heads):
            o_h = o_ref[_head_rows(o_ref, h, tt, n_heads)]
            a_h = (o_h * sg[:, :, h * hd:(h + 1) * hd]).reshape(bb * tt, hd).astype(BF16)
            r = r + _dot(a_h, w_ref[h * hd:(h + 1) * hd, :])
    else:
        a = (o_ref[...].astype(F32) * sg).reshape(bb * tt, aw).astype(BF16)
        r = _dot(a, w_ref[...])
    res_gate = mod_ref[...][:, :, 2 * d:3 * d]
    y_ref[...] = x_ref[...] + res_gate * _rms(r.reshape(bb, tt, d), gpost_ref[...])


def _out_proj(o, gate, x, mod, g_post, w, bb, tt, head_split):
    b, t, d = x.shape
    aw = gate.shape[-1]
    tile = lambda i, j: (i, j, 0)
    if head_split:
        o_spec = pl.BlockSpec((bb, o.shape[1] // t * tt, o.shape[2]), tile)
    else:
        o_spec = pl.BlockSpec((bb, tt, aw), tile)
    return pl.pallas_call(
        functools.partial(_out_kernel, head_split=head_split),
        grid=(b // bb, t // tt),
        in_specs=[
            o_spec,
            pl.BlockSpec((bb, tt, aw), tile),
            pl.BlockSpec((bb, tt, d), tile),
            pl.BlockSpec((bb, 1, mod.shape[-1]), lambda i, j: (i, 0, 0)),
            pl.BlockSpec((1, d), lambda i, j: (0, 0)),
            pl.BlockSpec((aw, d), lambda i, j: (0, 0)),
        ],
        out_specs=pl.BlockSpec((bb, tt, d), tile),
        out_shape=jax.ShapeDtypeStruct((b, t, d), F32),
        compiler_params=_params(2),
        name="out_proj",
    )(o, gate, x, mod, g_post.reshape(1, d), w)


def _select_blocks_t(gate_t, n_past):
    nb = gate_t.shape[0]
    blk = lax.broadcasted_iota(jnp.int32, gate_t.shape, 0)
    rank = jnp.zeros(gate_t.shape, jnp.int32)
    for m in range(nb):
        gm = gate_t[m:m + 1, :]
        ahead = jnp.where(gm > gate_t, 1, jnp.where(gm == gate_t, jnp.where(m < blk, 1, 0), 0))
        rank = rank + jnp.where(m < n_past, ahead, 0)
    return jnp.where(blk < n_past, jnp.where(rank < MOBA_TOPK, 1, 0), 0)


def _moba_prompt_kernel(q_ref, k_ref, vt_ref, kmean_ref, o_ref, shift_scr, s_scr, *, scale, hd):
    i = pl.program_id(2)
    g_heads = q_ref.shape[1] // hd
    c = scale * LOG2E
    ones_rows = jnp.ones((SUM_ROWS, MOBA_BLOCK), BF16)

    def values_and_sums(g, n, p):
        return _dot(jnp.concatenate([vt_ref[g, n], ones_rows], axis=0), p.astype(BF16))

    start = pl.multiple_of(i * MOBA_BLOCK, MOBA_BLOCK)
    key_i = lax.broadcasted_iota(jnp.int32, (MOBA_BLOCK, MOBA_BLOCK), 0)
    qry_i = lax.broadcasted_iota(jnp.int32, (MOBA_BLOCK, MOBA_BLOCK), 1)
    causal = key_i <= qry_i

    q_ts, s_owns, m0s = [], [], []
    for g in range(g_heads):
        ln = slice(g * hd, (g + 1) * hd)
        q_t = q_ref[:, ln].astype(F32).T.astype(BF16)
        gate_t = _dot(kmean_ref[:, ln].astype(BF16), q_t)
        keep = _select_blocks_t(gate_t, i)
        shift_scr[g] = jnp.where(keep > 0, 0.0, NEG_INF)
        s = jnp.where(causal, _dot(k_ref[pl.ds(start, MOBA_BLOCK), ln], q_t) * c, NEG_INF)
        q_ts.append(q_t)
        s_owns.append(s)
        m0s.append(jnp.max(s, axis=0, keepdims=True))

    per_iter = PROMPT_BLOCKS_PER_ITER
    n_full = i // per_iter
    n_rem = i - n_full * per_iter

    def over_past_blocks(blocks_fn, carry):
        carry = lax.fori_loop(0, n_full, lambda t, cr: blocks_fn(t * per_iter, per_iter, cr), carry)
        for r in range(1, per_iter):
            carry = lax.cond(n_rem >= r, lambda cr, r=r: blocks_fn(n_full * per_iter + (r - 1), 1, cr),
                             lambda cr: cr, carry)
        return carry

    def scores_blocks(heads, n0, count, ms):
        out = list(ms)
        for u in range(count):
            n = n0 + u
            st = pl.multiple_of(n * MOBA_BLOCK, MOBA_BLOCK)
            for k, g in enumerate(heads):
                sn = _dot(k_ref[pl.ds(st, MOBA_BLOCK), g * hd:(g + 1) * hd], q_ts[g]) * c
                s_scr[g, n] = sn
                out[k] = jnp.maximum(out[k], jnp.max(sn, axis=0, keepdims=True) + shift_scr[g, pl.ds(n, 1), :])
        return tuple(out)

    def values_blocks(heads, ms, n0, count, accs):
        out = list(accs)
        for u in range(count):
            n = n0 + u
            for k, g in enumerate(heads):
                pn = jnp.exp2(s_scr[g, n] - (ms[k] - shift_scr[g, pl.ds(n, 1), :]))
                out[k] = out[k] + values_and_sums(g, n, pn)
        return tuple(out)

    def own_values(heads, ms):
        return tuple(values_and_sums(g, i, jnp.exp2(s_owns[g] - ms[k])) for k, g in enumerate(heads))

    heads = tuple(range(g_heads))
    ms = over_past_blocks(lambda n0, cnt, cr: scores_blocks(heads, n0, cnt, cr), tuple(m0s))
    accs = over_past_blocks(lambda n0, cnt, cr: values_blocks(heads, ms, n0, cnt, cr), own_values(heads, ms))
    for g in heads:
        o_ref[:, g * hd:(g + 1) * hd] = (accs[g][0:hd, :] / accs[g][hd:hd + 1, :]).T.astype(o_ref.dtype)


def _moba_prompt(q, kb, vt, kmean, n_heads, hd, g_heads):
    b, s, aw = q.shape
    n_blk = s // MOBA_BLOCK
    gw = g_heads * hd
    return pl.pallas_call(
        functools.partial(_moba_prompt_kernel, scale=hd ** -0.5, hd=hd),
        grid=(b, n_heads // g_heads, n_blk),
        in_specs=[
            pl.BlockSpec((None, MOBA_BLOCK, gw), lambda bi, h, i: (bi, i, h)),
            pl.BlockSpec((None, s, gw), lambda bi, h, i: (bi, 0, h), pipeline_mode=pl.Buffered(1)),
            pl.BlockSpec((None, g_heads, n_blk, hd, MOBA_BLOCK), lambda bi, h, i: (bi, h, 0, 0, 0),
                         pipeline_mode=pl.Buffered(1)),
            pl.BlockSpec((None, n_blk, gw), lambda bi, h, i: (bi, 0, h)),
        ],
        out_specs=pl.BlockSpec((None, MOBA_BLOCK, gw), lambda bi, h, i: (bi, i, h)),
        out_shape=jax.ShapeDtypeStruct((b, s, aw), BF16),
        scratch_shapes=[
            pltpu.VMEM((g_heads, n_blk, MOBA_BLOCK), F32),
            pltpu.VMEM((g_heads, n_blk, MOBA_BLOCK, MOBA_BLOCK), F32),
        ],
        compiler_params=_params(3),
        name="moba_prompt",
    )(q, kb, vt, kmean)


def _moba_sample_kernel(pt_ref, q_ref, kn_ref, vn_ref, expand_ref, ck_ref, cv_ref, o_ref,
                        kmean_scr, s_scr, bmax_scr, shift_scr, l_scr, acc_scr, ring, sems,
                        *, scale, n_pages, n_heads):
    seq = pl.program_id(0)
    j = pl.program_id(1)
    n_slots, pps, pr, hd = ring.shape
    ks = n_pages // pps
    steps = 2 * ks
    look = n_slots - 1
    rows = q_ref.shape[0]
    t_new = rows // n_heads
    page = pr // n_heads

    def page_copy(cache_ref, page_id, slot, a):
        return pltpu.make_async_copy(cache_ref.at[page_id], ring.at[slot, a], sems.at[slot])

    def start_step(seq_i, j_i, slot):
        is_k = j_i < ks
        first = jnp.where(is_k, j_i, j_i - ks) * pps
        for a in range(pps):
            page_id = pt_ref[seq_i, first + a]

            @pl.when(is_k)
            def _():
                page_copy(ck_ref, page_id, slot, a).start()

            @pl.when(jnp.logical_not(is_k))
            def _():
                page_copy(cv_ref, page_id, slot, a).start()

    @pl.when(jnp.logical_and(seq == 0, j == 0))
    def _():
        for ahead in range(look):
            start_step(0, ahead, ahead)

    g = seq * steps + j
    j_next = j + look
    wrap = j_next >= steps
    seq_next = jnp.where(wrap, seq + 1, seq)

    @pl.when(seq_next < pl.num_programs(0))
    def _():
        start_step(seq_next, jnp.where(wrap, j_next - steps, j_next), lax.rem(g + look, n_slots))

    slot = lax.rem(g, n_slots)
    for a in range(pps):
        page_copy(ck_ref, 0, slot, a).wait()
    ppb = MOBA_BLOCK // page
    n_blocks = n_pages // ppb
    bps = pps // ppb
    lanes = kmean_scr.shape[0]
    c = scale * LOG2E
    q_all = q_ref[...].astype(BF16)

    r_i = lax.broadcasted_iota(jnp.int32, (rows, 1), 0)
    c_i = lax.broadcasted_iota(jnp.int32, (1, lanes), 1)
    r_tok = _bucket(r_i, n_heads, t_new)
    c_grp = _bucket(c_i, n_heads, lanes // n_heads)
    same_head = (r_i - n_heads * r_tok) == (c_i - n_heads * c_grp)

    def page_wide(tile):
        return jnp.concatenate([tile] * (pr // lanes), axis=1)

    @pl.when(j == 0)
    def _():
        kmean_scr[...] = jnp.zeros_like(kmean_scr)

    @pl.when(j < ks)
    def _():
        head_bias = page_wide(jnp.where(same_head, 0.0, NEG_INF))
        sums, maxes = [], []
        for a in range(pps):
            kp = ring[slot, a]
            sums.append(jnp.sum(kp.reshape(page, n_heads, hd), axis=0))
            s = _dot_nt(q_all, kp.astype(BF16))
            s_scr[j * pps + a] = s
            maxes.append(jnp.max(s + head_bias, axis=-1, keepdims=True))
        for b in range(bps):
            blk_sum, blk_max = sums[b * ppb], maxes[b * ppb]
            for e in range(1, ppb):
                blk_sum = blk_sum + sums[b * ppb + e]
                blk_max = jnp.maximum(blk_max, maxes[b * ppb + e])
            r0 = pl.multiple_of((j * bps + b) * n_heads, n_heads)
            kmean_scr[pl.ds(r0, n_heads), :] = blk_sum * (1.0 / MOBA_BLOCK)
            bmax_scr[j * bps + b] = jnp.broadcast_to(blk_max, (rows, lanes))

    @pl.when(j == ks - 1)
    def _():
        q_pad = jnp.concatenate([q_all, jnp.zeros((lanes - rows, hd), BF16)], axis=0)
        gate_t = _dot_nt(kmean_scr[...].astype(BF16), q_pad)
        tiles = [gate_t[n * n_heads:(n + 1) * n_heads, :] for n in range(n_blocks)]
        sub_i = lax.broadcasted_iota(jnp.int32, (n_heads, lanes), 0)
        own = jnp.where(c_i < rows, jnp.where(sub_i == c_i - n_heads * c_grp, 1.0, 0.0), 0.0)
        keep_tiles = []
        for n in range(n_blocks):
            rank = jnp.zeros((n_heads, lanes), jnp.int32)
            for m in range(n_blocks):
                if m < n:
                    rank = rank + jnp.where(tiles[m] >= tiles[n], 1, 0)
                elif m > n:
                    rank = rank + jnp.where(tiles[m] > tiles[n], 1, 0)
            keep_tiles.append(jnp.where(rank < MOBA_TOPK, own, 0.0))
        keep_tiles.append(jnp.zeros((lanes - n_blocks * n_heads, lanes), F32))
        keep = jnp.concatenate(keep_tiles, axis=0).T
        keep_x = _dot(keep.astype(BF16), expand_ref[...])[0:rows, :]

        pad = jnp.zeros((lanes - rows, hd), F32)
        kn = jnp.concatenate([kn_ref[...], pad], axis=0).astype(BF16)
        vn = jnp.concatenate([vn_ref[...], pad], axis=0).astype(BF16)
        own_ok = jnp.where(c_i < rows, jnp.where(same_head, jnp.where(c_grp <= r_tok, 1, 0), 0), 0)
        s_own = jnp.where(own_ok > 0, _dot_nt(q_all, kn), NEG_INF)

        kept = [keep_x[:, n * lanes:(n + 1) * lanes] > 0.5 for n in range(n_blocks)]
        m_lane = jnp.full((rows, lanes), NEG_INF, F32)
        for n in range(n_blocks):
            m_lane = jnp.maximum(m_lane, jnp.where(kept[n], bmax_scr[n], NEG_INF))
        m_fin = jnp.maximum(jnp.max(s_own, axis=-1, keepdims=True),
                            jnp.max(m_lane, axis=-1, keepdims=True))
        neg_m = jnp.broadcast_to(-m_fin, (rows, lanes))
        for n in range(n_blocks):
            shift_scr[n] = jnp.where(kept[n], neg_m, NEG_INF)
        p_own = jnp.exp2((s_own - m_fin) * c)
        l_scr[...] = jnp.broadcast_to(jnp.sum(p_own, axis=-1, keepdims=True), l_scr.shape)
        acc_scr[...] = _dot(p_own.astype(BF16), vn)

    @pl.when(j >= ks)
    def _():
        acc = acc_scr[...]
        l_add = jnp.zeros((rows, 1), F32)
        for b in range(bps):
            shift = page_wide(shift_scr[(j - ks) * bps + b])
            for e in range(ppb):
                a = b * ppb + e
                pn = jnp.exp2((s_scr[(j - ks) * pps + a] + shift) * c)
                l_add = l_add + jnp.sum(pn, axis=-1, keepdims=True)
                acc = acc + _dot(pn.astype(BF16), ring[slot, a].astype(BF16))
        acc_scr[...] = acc
        l_scr[...] = l_scr[...] + l_add

    @pl.when(j == 2 * ks - 1)
    def _():
        o_ref[...] = acc_scr[...] / l_scr[...]


def _moba_sample(q, k_new, v_new, cache_k, cache_v, page_table, expand):
    n_phys, page, n_heads, hd = cache_k.shape
    db, t_new = q.shape[0], q.shape[1] // n_heads
    cache_k = cache_k.reshape(n_phys, page * n_heads, hd)
    cache_v = cache_v.reshape(n_phys, page * n_heads, hd)
    n_pages = page_table.shape[1]
    pps = SAMPLE_PAGES_PER_STEP
    n_slots = SAMPLE_RING_SLOTS
    assert MOBA_BLOCK % page == 0 and pps % (MOBA_BLOCK // page) == 0 and n_pages % pps == 0
    ks = n_pages // pps
    assert n_slots - 1 <= 2 * ks
    rows = t_new * n_heads
    lanes = V7X_LANES
    assert rows <= lanes and (n_pages * page // MOBA_BLOCK) * n_heads <= lanes and lanes % n_heads == 0
    assert (page * n_heads) % lanes == 0

    seq = lambda s, j, pt: (s, 0, 0)
    new_spec = pl.BlockSpec((None, rows, hd), seq)
    grid_spec = pltpu.PrefetchScalarGridSpec(
        num_scalar_prefetch=1,
        grid=(db, 2 * ks),
        in_specs=[new_spec, new_spec, new_spec, pl.BlockSpec(expand.shape, lambda s, j, pt: (0, 0)),
                  pl.BlockSpec(memory_space=pl.ANY), pl.BlockSpec(memory_space=pl.ANY)],
        out_specs=new_spec,
        scratch_shapes=[
            pltpu.VMEM((lanes, hd), F32),
            pltpu.VMEM((n_pages, rows, page * n_heads), F32),
            pltpu.VMEM((n_pages * page // MOBA_BLOCK, rows, lanes), F32),
            pltpu.VMEM((n_pages * page // MOBA_BLOCK, rows, lanes), F32),
            pltpu.VMEM((rows, hd), F32),
            pltpu.VMEM((rows, hd), F32),
            pltpu.VMEM((n_slots, pps, page * n_heads, hd), F32),
            pltpu.SemaphoreType.DMA((n_slots,)),
        ],
    )
    return pl.pallas_call(
        functools.partial(_moba_sample_kernel, scale=hd ** -0.5, n_pages=n_pages, n_heads=n_heads),
        grid_spec=grid_spec,
        out_shape=jax.ShapeDtypeStruct((db, rows, hd), F32),
        compiler_params=_params(2),
        name="moba_sample",
    )(page_table, q, k_new, v_new, expand, cache_k, cache_v)


def _expand_matrix(n_blocks, n_heads, lanes):
    r = jnp.arange(lanes)[:, None]
    c = jnp.arange(n_blocks * lanes)[None, :]
    return ((r // n_heads == c // lanes) & (r % n_heads == c % n_heads) & (r < n_blocks * n_heads)).astype(BF16)


def _trunk(x, mods, mods_kv, hist, pos0, wts, n_heads, hd, bb, tt, attend, is_prompt):
    t = x.shape[1]
    n_a = wts["w_in_pool"].shape[0]
    n_b = wts["w_in_attn"].shape[0]
    cos, sin = _rope_tables(t, hd, pos0)
    new_hist = []
    for l in range(n_a):
        x, hn = _pool_layer(x, mods[l], hist[l], wts["g_pre"][l], wts["g_post"][l], wts["w_in_pool"][l],
                            wts["w_pool_group"][l], wts["pool_scale"][l], wts["w_out_pool"][l], pos0, bb, tt)
        new_hist.append(hn)
    kv = _kv_proj(x, mods_kv, wts["g_kv"], wts["w_kv"], cos, sin, n_heads, hd, bb, tt, is_prompt)
    for jl in range(n_b):
        l = n_a + jl
        q, gate = _q_proj(x, mods[l], wts["g_pre"][l], wts["w_in_attn"][jl], cos, sin, n_heads, hd, bb, tt,
                          head_split=not is_prompt)
        o = attend(q, kv)
        x = _out_proj(o, gate, x, mods[l], wts["g_post"][l], wts["w_out_attn"][jl], bb, tt,
                      head_split=not is_prompt)
    b = x.shape[0]
    return x, kv[0].reshape(b, t, n_heads, hd), kv[1].reshape(b, t, n_heads, hd), jnp.stack(new_hist)


def kernel(x_prompt, x_sample, cache_k, cache_v, state_pool, page_table, c_prompt, c_sample, w_ada, b_ada,
           g_pre, g_post, w_in_pool, w_pool_group, pool_scale, w_out_pool, g_kv, w_ada_kv, b_ada_kv, w_kv,
           w_in_attn, w_out_attn):
    n_b_p = x_prompt.shape[0]
    page, n_heads, hd = cache_k.shape[1], cache_k.shape[2], cache_k.shape[3]
    past_len = page_table.shape[1] * page
    n_a = w_in_pool.shape[0]
    n_hist = state_pool.shape[2]

    c_all = jnp.concatenate([c_prompt, c_sample], axis=0)
    r = c_all.shape[0]
    r_pad = -(-r // V7X_SUBLANES) * V7X_SUBLANES
    c_all = jnp.pad(c_all, ((0, r_pad - r), (0, 0)))
    mods_all = _ada_mods(c_all, w_ada, b_ada)
    mods_kv_all = _ada_mods(c_all, w_ada_kv[None], b_ada_kv[None])[0]
    mods_p = [mods_all[l, 0:n_b_p][:, None, :] for l in range(mods_all.shape[0])]
    mods_s = [mods_all[l, n_b_p:r][:, None, :] for l in range(mods_all.shape[0])]
    mods_kv_p = mods_kv_all[0:n_b_p][:, None, :]
    mods_kv_s = mods_kv_all[n_b_p:r][:, None, :]

    wts = dict(
        g_pre=g_pre, g_post=g_post, pool_scale=pool_scale, g_kv=g_kv,
        w_in_pool=w_in_pool.astype(BF16), w_pool_group=w_pool_group.astype(BF16),
        w_out_pool=w_out_pool.astype(BF16), w_kv=w_kv.astype(BF16),
        w_in_attn=w_in_attn.astype(BF16), w_out_attn=w_out_attn.astype(BF16))

    def attend_prompt(q, kv):
        _, _, kb, vt, kmean = kv
        kmean = kmean.reshape(kmean.shape[0], -1, kmean.shape[-1])
        return _moba_prompt(q, kb, vt, kmean, n_heads, hd, g_heads=PROMPT_HEADS_PER_STEP)

    expand = _expand_matrix(past_len // MOBA_BLOCK, n_heads, V7X_LANES)

    def attend_sample(q, kv):
        return _moba_sample(q, kv[0], kv[1], cache_k, cache_v, page_table, expand)

    hist0 = jnp.zeros((n_a, n_b_p, n_hist, state_pool.shape[3]), F32)
    y_p, k_p, v_p, pool_p = _trunk(x_prompt, mods_p, mods_kv_p, hist0, 0, wts, n_heads, hd,
                                   bb=1, tt=PROMPT_TOKENS_PER_STEP, attend=attend_prompt, is_prompt=True)
    y_s, k_s, v_s, pool_s = _trunk(x_sample, mods_s, mods_kv_s, state_pool, past_len, wts, n_heads, hd,
                                   bb=SAMPLE_SEQS_PER_STEP, tt=x_sample.shape[1], attend=attend_sample,
                                   is_prompt=False)
    return (y_p, y_s, k_p, v_p, k_s, v_s, pool_p, pool_s)
```

```python
import functools
import math

import jax
import jax.numpy as jnp
from jax import lax
from jax.experimental import pallas as pl
from jax.experimental.pallas import tpu as pltpu

F32 = jnp.float32
BF16 = jnp.bfloat16
NEG_INF = float("-inf")
LOG2E = math.log2(math.e)

POOL_WINDOWS = (2, 4, 8, 16)
MOBA_BLOCK = 256
MOBA_TOPK = 3
ROPE_THETA = 10000.0
NORM_EPS = 1e-6

V7X_LANES = 128
V7X_SUBLANES = 8
V7X_VMEM_LIMIT_BYTES = 56 * 1024 * 1024

HIST_PAD = 16
SUM_ROWS = 16
PROMPT_HEADS_PER_STEP = 8
PROMPT_BLOCKS_PER_ITER = 4
SAMPLE_PAGES_PER_STEP = 16
SAMPLE_RING_SLOTS = 3
PROMPT_TOKENS_PER_STEP = 512
SAMPLE_SEQS_PER_STEP = 64


def _params(n_axes):
    return pltpu.CompilerParams(
        dimension_semantics=("arbitrary",) * n_axes, vmem_limit_bytes=V7X_VMEM_LIMIT_BYTES)


def _silu(x):
    return x * (1.0 / (1.0 + jnp.exp(-x)))


def _rms(x, g):
    return x * lax.rsqrt(jnp.mean(x * x, axis=-1, keepdims=True) + NORM_EPS) * g


def _dot(a, b):
    return jnp.dot(a, b, preferred_element_type=F32)


def _bucket(idx, size, n):
    out = jnp.zeros_like(idx)
    for h in range(1, n):
        out = out + jnp.where(idx >= h * size, 1, 0)
    return out


def _dot_nt(a, b):
    return lax.dot_general(a, b, (((1,), (1,)), ((), ())), preferred_element_type=F32)


def _rope_table_kernel(cos_ref, sin_ref, *, pos0):
    t, hd = cos_ref.shape
    half = hd // 2
    lane = lax.broadcasted_iota(jnp.int32, (t, hd), 1)
    row = lax.broadcasted_iota(jnp.int32, (t, hd), 0)
    j = jnp.where(lane < half, lane, lane - half).astype(F32)
    inv = jnp.exp(j * (-2.0 * math.log(ROPE_THETA) / hd))
    ang = (row + pos0).astype(F32) * inv
    cos_ref[...] = jnp.cos(ang)
    s = jnp.sin(ang)
    sin_ref[...] = jnp.where(lane < half, -s, s)


def _rope_tables(t, hd, pos0):
    return pl.pallas_call(
        functools.partial(_rope_table_kernel, pos0=pos0),
        out_shape=(jax.ShapeDtypeStruct((t, hd), F32), jax.ShapeDtypeStruct((t, hd), F32)),
        name="rope_tables",
    )()


def _head_rows(ref, h, n_tok, n_heads):
    return (slice(None),) * (len(ref.shape) - 2) + (pl.ds(h, n_tok, stride=n_heads), slice(None))


def _rope_head(a, h, cos, sin, bb, tt, hd):
    slab = a[:, h * hd:(h + 1) * hd]
    rot = pltpu.roll(slab, hd // 2, 1)
    return slab.reshape(bb, tt, hd) * cos[None] + rot.reshape(bb, tt, hd) * sin[None]


def _ada_kernel(c_ref, w_ref, b_ref, o_ref):
    h = _silu(c_ref[...]).astype(BF16)
    o_ref[...] = _dot(h, w_ref[...].astype(BF16)) + b_ref[...]


def _ada_mods(c, w, b):
    n_l, d, n = w.shape
    r = c.shape[0]
    bn = 1024
    return pl.pallas_call(
        _ada_kernel,
        grid=(n_l, n // bn),
        in_specs=[
            pl.BlockSpec((r, d), lambda l, j: (0, 0)),
            pl.BlockSpec((None, d, bn), lambda l, j: (l, 0, j)),
            pl.BlockSpec((None, 1, bn), lambda l, j: (l, 0, j)),
        ],
        out_specs=pl.BlockSpec((None, r, bn), lambda l, j: (l, 0, j)),
        out_shape=jax.ShapeDtypeStruct((n_l, r, n), F32),
        compiler_params=_params(2),
        name="ada_mods",
    )(c, w, b.reshape(n_l, 1, n))


def _pool_kernel(x_ref, mod_ref, hist_ref, gpre_ref, gpost_ref, win_ref, wgrp_ref, pscale_ref, wout_ref,
                 y_ref, hist_out_ref, z_scr, *, pos0):
    t = pl.program_id(1)
    n_t = pl.num_programs(1)
    bb, tt, d = x_ref.shape
    w = z_scr.shape[-1]
    n_hist = hist_ref.shape[1]
    m = bb * tt
    grp = w // len(POOL_WINDOWS)

    @pl.when(t == 0)
    def _():
        z_scr[:, HIST_PAD - n_hist:HIST_PAD, :] = hist_ref[...]

    @pl.when(t > 0)
    def _():
        z_scr[:, 0:HIST_PAD, :] = z_scr[:, tt:tt + HIST_PAD, :]

    mod = mod_ref[...]
    shift, scale, gate = mod[:, :, 0:d], mod[:, :, d:2 * d], mod[:, :, 2 * d:3 * d]
    x = x_ref[...]
    h = _rms(x, gpre_ref[...]) * (1.0 + scale) + shift
    ug = _dot(h.reshape(m, d).astype(BF16), win_ref[...])
    u = ug[:, 0:w]
    z_scr[:, HIST_PAD:HIST_PAD + tt, :] = u.reshape(bb, tt, w)

    pos = pos0 + t * tt + lax.broadcasted_iota(jnp.int32, (1, tt, 1), 1)
    ys = []
    for g, win in enumerate(POOL_WINDOWS):
        sl = slice(g * grp, (g + 1) * grp)
        acc = z_scr[:, HIST_PAD:HIST_PAD + tt, sl]
        for k in range(1, win):
            acc = acc + z_scr[:, HIST_PAD - k:HIST_PAD - k + tt, sl]
        inv_cnt = 1.0 / jnp.minimum(pos + 1, win).astype(F32)
        dm = acc * inv_cnt - z_scr[:, HIST_PAD:HIST_PAD + tt, sl]
        ys.append(_dot(dm.reshape(m, grp).astype(BF16), wgrp_ref[g]))
    y = jnp.concatenate(ys, axis=-1) * pscale_ref[...]
    a = (y * _silu(ug[:, w:2 * w])).astype(BF16)
    o = _dot(a, wout_ref[...]).reshape(bb, tt, d)
    y_ref[...] = x + gate * _rms(o, gpost_ref[...])

    @pl.when(t == n_t - 1)
    def _():
        hist_out_ref[...] = z_scr[:, HIST_PAD + tt - n_hist:HIST_PAD + tt, :]


def _pool_layer(x, mod, hist, g_pre, g_post, w_in, w_grp, pool_scale, w_out, pos0, bb, tt):
    b, t, d = x.shape
    w = w_out.shape[0]
    n_hist = hist.shape[1]
    n_g, grp, _ = w_grp.shape
    const2 = lambda i, j: (0, 0)
    return pl.pallas_call(
        functools.partial(_pool_kernel, pos0=pos0),
        grid=(b // bb, t // tt),
        in_specs=[
            pl.BlockSpec((bb, tt, d), lambda i, j: (i, j, 0)),
            pl.BlockSpec((bb, 1, 3 * d), lambda i, j: (i, 0, 0)),
            pl.BlockSpec((bb, n_hist, w), lambda i, j: (i, 0, 0)),
            pl.BlockSpec((1, d), const2),
            pl.BlockSpec((1, d), const2),
            pl.BlockSpec((d, 2 * w), const2),
            pl.BlockSpec((n_g, grp, grp), lambda i, j: (0, 0, 0)),
            pl.BlockSpec((1, w), const2),
            pl.BlockSpec((w, d), const2),
        ],
        out_specs=(
            pl.BlockSpec((bb, tt, d), lambda i, j: (i, j, 0)),
            pl.BlockSpec((bb, n_hist, w), lambda i, j: (i, 0, 0)),
        ),
        out_shape=(jax.ShapeDtypeStruct((b, t, d), F32), jax.ShapeDtypeStruct((b, n_hist, w), F32)),
        scratch_shapes=[pltpu.VMEM((bb, HIST_PAD + tt, w), F32)],
        compiler_params=_params(2),
        name="pool_layer",
    )(x, mod, hist, g_pre.reshape(1, d), g_post.reshape(1, d), w_in, w_grp, pool_scale.reshape(1, w), w_out)


def _normed(x_ref, mod_ref, g_ref):
    bb, tt, d = x_ref.shape
    mod = mod_ref[...]
    shift, scale = mod[:, :, 0:d], mod[:, :, d:2 * d]
    h = _rms(x_ref[...], g_ref[...]) * (1.0 + scale) + shift
    return h.reshape(bb * tt, d).astype(BF16)


def _kv_kernel(x_ref, mod_ref, g_ref, w_ref, cos_ref, sin_ref, k_ref, v_ref, *extra, n_heads, hd):
    bb, tt, _ = x_ref.shape
    kvw = n_heads * hd
    kv = _dot(_normed(x_ref, mod_ref, g_ref), w_ref[...])
    cos, sin = cos_ref[...], sin_ref[...]
    n_blk = tt // MOBA_BLOCK
    for h in range(n_heads):
        k_h = _rope_head(kv, h, cos, sin, bb, tt, hd)
        v_h = kv[:, kvw + h * hd:kvw + (h + 1) * hd]
        k_ref[_head_rows(k_ref, h, tt, n_heads)] = k_h
        v_ref[_head_rows(v_ref, h, tt, n_heads)] = v_h.reshape(bb, tt, hd)
        if extra:
            kb_ref, vt_ref, kmean_ref = extra
            kb_ref[:, :, h * hd:(h + 1) * hd] = k_h.astype(BF16)
            kmean_ref[:, h * hd:(h + 1) * hd] = jnp.mean(k_h.reshape(n_blk, MOBA_BLOCK, hd), axis=1)
            for j in range(n_blk):
                vt_ref[h, j] = v_h[j * MOBA_BLOCK:(j + 1) * MOBA_BLOCK, :].T.astype(BF16)


def _kv_proj(x, mod, g, w, cos, sin, n_heads, hd, bb, tt, with_attn_layouts):
    b, t, d = x.shape
    kvw = n_heads * hd
    const2 = lambda i, j: (0, 0)
    tile = lambda i, j: (i, j, 0)
    out_specs = [pl.BlockSpec((bb, tt * n_heads, hd), tile), pl.BlockSpec((bb, tt * n_heads, hd), tile)]
    out_shape = [jax.ShapeDtypeStruct((b, t * n_heads, hd), F32), jax.ShapeDtypeStruct((b, t * n_heads, hd), F32)]
    if with_attn_layouts:
        assert bb == 1 and tt % MOBA_BLOCK == 0
        n_blk = tt // MOBA_BLOCK
        out_specs += [
            pl.BlockSpec((bb, tt, kvw), tile),
            pl.BlockSpec((None, n_heads, n_blk, hd, MOBA_BLOCK), lambda i, j: (i, 0, j, 0, 0)),
            pl.BlockSpec((None, None, n_blk, kvw), lambda i, j: (i, j, 0, 0)),
        ]
        out_shape += [
            jax.ShapeDtypeStruct((b, t, kvw), BF16),
            jax.ShapeDtypeStruct((b, n_heads, t // MOBA_BLOCK, hd, MOBA_BLOCK), BF16),
            jax.ShapeDtypeStruct((b, t // tt, n_blk, kvw), F32),
        ]
    return pl.pallas_call(
        functools.partial(_kv_kernel, n_heads=n_heads, hd=hd),
        grid=(b // bb, t // tt),
        in_specs=[
            pl.BlockSpec((bb, tt, d), tile),
            pl.BlockSpec((bb, 1, mod.shape[-1]), lambda i, j: (i, 0, 0)),
            pl.BlockSpec((1, d), const2),
            pl.BlockSpec((d, 2 * kvw), const2),
            pl.BlockSpec((tt, hd), lambda i, j: (j, 0)),
            pl.BlockSpec((tt, hd), lambda i, j: (j, 0)),
        ],
        out_specs=tuple(out_specs),
        out_shape=tuple(out_shape),
        compiler_params=_params(2),
        name="kv_proj",
    )(x, mod, g.reshape(1, d), w, cos, sin)


def _q_kernel(x_ref, mod_ref, g_ref, w_ref, cos_ref, sin_ref, q_ref, gate_ref, *, n_heads, hd, head_split):
    bb, tt, _ = x_ref.shape
    aw = n_heads * hd
    qg = _dot(_normed(x_ref, mod_ref, g_ref), w_ref[...])
    cos, sin = cos_ref[...], sin_ref[...]
    for h in range(n_heads):
        q_h = _rope_head(qg, h, cos, sin, bb, tt, hd)
        if head_split:
            q_ref[_head_rows(q_ref, h, tt, n_heads)] = q_h
        else:
            q_ref[:, :, h * hd:(h + 1) * hd] = q_h.astype(BF16)
    gate_ref[...] = qg[:, aw:2 * aw].reshape(bb, tt, aw).astype(gate_ref.dtype)


def _q_proj(x, mod, g, w, cos, sin, n_heads, hd, bb, tt, head_split):
    b, t, d = x.shape
    aw = n_heads * hd
    const2 = lambda i, j: (0, 0)
    tile = lambda i, j: (i, j, 0)
    if head_split:
        q_spec = pl.BlockSpec((bb, tt * n_heads, hd), tile)
        q_shape = jax.ShapeDtypeStruct((b, t * n_heads, hd), F32)
    else:
        q_spec = pl.BlockSpec((bb, tt, aw), tile)
        q_shape = jax.ShapeDtypeStruct((b, t, aw), BF16)
    return pl.pallas_call(
        functools.partial(_q_kernel, n_heads=n_heads, hd=hd, head_split=head_split),
        grid=(b // bb, t // tt),
        in_specs=[
            pl.BlockSpec((bb, tt, d), tile),
            pl.BlockSpec((bb, 1, mod.shape[-1]), lambda i, j: (i, 0, 0)),
            pl.BlockSpec((1, d), const2),
            pl.BlockSpec((d, 2 * aw), const2),
            pl.BlockSpec((tt, hd), lambda i, j: (j, 0)),
            pl.BlockSpec((tt, hd), lambda i, j: (j, 0)),
        ],
        out_specs=(q_spec, pl.BlockSpec((bb, tt, aw), tile)),
        out_shape=(q_shape, jax.ShapeDtypeStruct((b, t, aw), q_shape.dtype)),
        compiler_params=_params(2),
        name="q_proj",
    )(x, mod, g.reshape(1, d), w, cos, sin)


def _out_kernel(o_ref, gate_ref, x_ref, mod_ref, gpost_ref, w_ref, y_ref, *, head_split):
    bb, tt, d = x_ref.shape
    aw = gate_ref.shape[-1]
    sg = _silu(gate_ref[...].astype(F32))
    if head_split:
        hd = o_ref.shape[2]
        n_heads = aw // hd
        r = jnp.zeros((bb * tt, d), F32)
        for h in range(n_heads):
            o_h = o_ref[_head_rows(o_ref, h, tt, n_heads)]
            a_h = (o_h * sg[:, :, h * hd:(h + 1) * hd]).reshape(bb * tt, hd).astype(BF16)
            r = r + _dot(a_h, w_ref[h * hd:(h + 1) * hd, :])
    else:
        a = (o_ref[...].astype(F32) * sg).reshape(bb * tt, aw).astype(BF16)
        r = _dot(a, w_ref[...])
    res_gate = mod_ref[...][:, :, 2 * d:3 * d]
    y_ref[...] = x_ref[...] + res_gate * _rms(r.reshape(bb, tt, d), gpost_ref[...])


def _out_proj(o, gate, x, mod, g_post, w, bb, tt, head_split):
    b, t, d = x.shape
    aw = gate.shape[-1]
    tile = lambda i, j: (i, j, 0)
    if head_split:
        o_spec = pl.BlockSpec((bb, o.shape[1] // t * tt, o.shape[2]), tile)
    else:
        o_spec = pl.BlockSpec((bb, tt, aw), tile)
    return pl.pallas_call(
        functools.partial(_out_kernel, head_split=head_split),
        grid=(b // bb, t // tt),
        in_specs=[
            o_spec,
            pl.BlockSpec((bb, tt, aw), tile),
            pl.BlockSpec((bb, tt, d), tile),
            pl.BlockSpec((bb, 1, mod.shape[-1]), lambda i, j: (i, 0, 0)),
            pl.BlockSpec((1, d), lambda i, j: (0, 0)),
            pl.BlockSpec((aw, d), lambda i, j: (0, 0)),
        ],
        out_specs=pl.BlockSpec((bb, tt, d), tile),
        out_shape=jax.ShapeDtypeStruct((b, t, d), F32),
        compiler_params=_params(2),
        name="out_proj",
    )(o, gate, x, mod, g_post.reshape(1, d), w)


def _select_blocks_t(gate_t, n_past):
    nb = gate_t.shape[0]
    blk = lax.broadcasted_iota(jnp.int32, gate_t.shape, 0)
    g = jnp.where(blk < n_past, gate_t, NEG_INF)
    rank = jnp.zeros(gate_t.shape, jnp.int32)
    for m in range(nb):
        gm = g[m:m + 1, :]
        rank = rank + jnp.where(gm > g, 1, jnp.where(gm == g, jnp.where(m < blk, 1, 0), 0))
    return jnp.where(blk < n_past, jnp.where(rank < MOBA_TOPK, 1, 0), 0)


def _moba_prompt_kernel(q_ref, k_ref, vt_ref, kmean_ref, o_ref, shift_scr, s_scr, *, scale, hd):
    i = pl.program_id(2)
    g_heads = q_ref.shape[1] // hd
    c = scale * LOG2E
    ones_rows = jnp.ones((SUM_ROWS, MOBA_BLOCK), BF16)

    def values_and_sums(g, n, p):
        return _dot(jnp.concatenate([vt_ref[g, n], ones_rows], axis=0), p.astype(BF16))

    start = pl.multiple_of(i * MOBA_BLOCK, MOBA_BLOCK)
    key_i = lax.broadcasted_iota(jnp.int32, (MOBA_BLOCK, MOBA_BLOCK), 0)
    qry_i = lax.broadcasted_iota(jnp.int32, (MOBA_BLOCK, MOBA_BLOCK), 1)
    causal = key_i <= qry_i

    q_ts, s_owns, m0s = [], [], []
    for g in range(g_heads):
        ln = slice(g * hd, (g + 1) * hd)
        q_t = q_ref[:, ln].astype(F32).T.astype(BF16)
        gate_t = _dot(kmean_ref[:, ln].astype(BF16), q_t)
        keep = _select_blocks_t(gate_t, i)
        shift_scr[g] = jnp.where(keep > 0, 0.0, NEG_INF)
        s = jnp.where(causal, _dot(k_ref[pl.ds(start, MOBA_BLOCK), ln], q_t) * c, NEG_INF)
        q_ts.append(q_t)
        s_owns.append(s)
        m0s.append(jnp.max(s, axis=0, keepdims=True))

    per_iter = PROMPT_BLOCKS_PER_ITER
    n_full = i // per_iter
    n_rem = i - n_full * per_iter

    def over_past_blocks(blocks_fn, carry):
        carry = lax.fori_loop(0, n_full, lambda t, cr: blocks_fn(t * per_iter, per_iter, cr), carry)
        done = n_full * per_iter
        size = per_iter // 2
        while size >= 1:
            take = (n_rem & size) != 0
            carry = lax.cond(take, lambda cr, done=done, size=size: blocks_fn(done, size, cr), lambda cr: cr, carry)
            done = done + jnp.where(take, size, 0)
            size //= 2
        return carry

    def scores_blocks(heads, n0, count, ms):
        out = list(ms)
        for u in range(count):
            n = n0 + u
            st = pl.multiple_of(n * MOBA_BLOCK, MOBA_BLOCK)
            for k, g in enumerate(heads):
                sn = _dot(k_ref[pl.ds(st, MOBA_BLOCK), g * hd:(g + 1) * hd], q_ts[g]) * c
                s_scr[g, n] = sn
                out[k] = jnp.maximum(out[k], jnp.max(sn, axis=0, keepdims=True) + shift_scr[g, pl.ds(n, 1), :])
        return tuple(out)

    def values_blocks(heads, ms, n0, count, accs):
        out = list(accs)
        for u in range(count):
            n = n0 + u
            for k, g in enumerate(heads):
                pn = jnp.exp2(s_scr[g, n] - (ms[k] - shift_scr[g, pl.ds(n, 1), :]))
                out[k] = out[k] + values_and_sums(g, n, pn)
        return tuple(out)

    def own_values(heads, ms):
        return tuple(values_and_sums(g, i, jnp.exp2(s_owns[g] - ms[k])) for k, g in enumerate(heads))

    heads = tuple(range(g_heads))
    ms = over_past_blocks(lambda n0, cnt, cr: scores_blocks(heads, n0, cnt, cr), tuple(m0s))
    accs = over_past_blocks(lambda n0, cnt, cr: values_blocks(heads, ms, n0, cnt, cr), own_values(heads, ms))
    for g in heads:
        o_ref[:, g * hd:(g + 1) * hd] = (accs[g][0:hd, :] / accs[g][hd:hd + 1, :]).T.astype(o_ref.dtype)


def _moba_prompt(q, kb, vt, kmean, n_heads, hd, g_heads):
    b, s, aw = q.shape
    n_blk = s // MOBA_BLOCK
    gw = g_heads * hd
    return pl.pallas_call(
        functools.partial(_moba_prompt_kernel, scale=hd ** -0.5, hd=hd),
        grid=(b, n_heads // g_heads, n_blk),
        in_specs=[
            pl.BlockSpec((None, MOBA_BLOCK, gw), lambda bi, h, i: (bi, i, h)),
            pl.BlockSpec((None, s, gw), lambda bi, h, i: (bi, 0, h), pipeline_mode=pl.Buffered(1)),
            pl.BlockSpec((None, g_heads, n_blk, hd, MOBA_BLOCK), lambda bi, h, i: (bi, h, 0, 0, 0),
                         pipeline_mode=pl.Buffered(1)),
            pl.BlockSpec((None, n_blk, gw), lambda bi, h, i: (bi, 0, h)),
        ],
        out_specs=pl.BlockSpec((None, MOBA_BLOCK, gw), lambda bi, h, i: (bi, i, h)),
        out_shape=jax.ShapeDtypeStruct((b, s, aw), BF16),
        scratch_shapes=[
            pltpu.VMEM((g_heads, n_blk, MOBA_BLOCK), F32),
            pltpu.VMEM((g_heads, n_blk, MOBA_BLOCK, MOBA_BLOCK), F32),
        ],
        compiler_params=_params(3),
        name="moba_prompt",
    )(q, kb, vt, kmean)


def _moba_sample_kernel(pt_ref, q_ref, kn_ref, vn_ref, expand_ref, ck_ref, cv_ref, o_ref,
                        kmean_scr, s_scr, bmax_scr, shift_scr, l_scr, acc_scr, ring, sems,
                        *, scale, n_pages, n_heads):
    seq = pl.program_id(0)
    j = pl.program_id(1)
    n_slots, pps, pr, hd = ring.shape
    ks = n_pages // pps
    steps = 2 * ks
    look = n_slots - 1
    rows = q_ref.shape[0]
    t_new = rows // n_heads
    page = pr // n_heads

    def page_copy(cache_ref, page_id, slot, a):
        return pltpu.make_async_copy(cache_ref.at[page_id], ring.at[slot, a], sems.at[slot])

    def start_step(seq_i, j_i, slot):
        is_k = j_i < ks
        first = jnp.where(is_k, j_i, j_i - ks) * pps
        for a in range(pps):
            page_id = pt_ref[seq_i, first + a]

            @pl.when(is_k)
            def _():
                page_copy(ck_ref, page_id, slot, a).start()

            @pl.when(jnp.logical_not(is_k))
            def _():
                page_copy(cv_ref, page_id, slot, a).start()

    @pl.when(jnp.logical_and(seq == 0, j == 0))
    def _():
        for ahead in range(look):
            start_step(0, ahead, ahead)

    g = seq * steps + j
    j_next = j + look
    wrap = j_next >= steps
    seq_next = jnp.where(wrap, seq + 1, seq)

    @pl.when(seq_next < pl.num_programs(0))
    def _():
        start_step(seq_next, jnp.where(wrap, j_next - steps, j_next), lax.rem(g + look, n_slots))

    slot = lax.rem(g, n_slots)
    for a in range(pps):
        page_copy(ck_ref, 0, slot, a).wait()
    ppb = MOBA_BLOCK // page
    n_blocks = n_pages // ppb
    bps = pps // ppb
    lanes = kmean_scr.shape[0]
    c = scale * LOG2E
    q_all = q_ref[...].astype(BF16)

    r_i = lax.broadcasted_iota(jnp.int32, (rows, 1), 0)
    c_i = lax.broadcasted_iota(jnp.int32, (1, lanes), 1)
    r_tok = _bucket(r_i, n_heads, t_new)
    c_grp = _bucket(c_i, n_heads, lanes // n_heads)
    same_head = (r_i - n_heads * r_tok) == (c_i - n_heads * c_grp)

    def page_wide(tile):
        return jnp.concatenate([tile] * (pr // lanes), axis=1)

    @pl.when(j == 0)
    def _():
        kmean_scr[...] = jnp.zeros_like(kmean_scr)

    @pl.when(j < ks)
    def _():
        head_bias = page_wide(jnp.where(same_head, 0.0, NEG_INF))
        sums, maxes = [], []
        for a in range(pps):
            kp = ring[slot, a]
            sums.append(jnp.sum(kp.reshape(page, n_heads, hd), axis=0))
            s = _dot_nt(q_all, kp.astype(BF16))
            s_scr[j * pps + a] = s
            maxes.append(jnp.max(s + head_bias, axis=-1, keepdims=True))
        for b in range(bps):
            blk_sum, blk_max = sums[b * ppb], maxes[b * ppb]
            for e in range(1, ppb):
                blk_sum = blk_sum + sums[b * ppb + e]
                blk_max = jnp.maximum(blk_max, maxes[b * ppb + e])
            r0 = pl.multiple_of((j * bps + b) * n_heads, n_heads)
            kmean_scr[pl.ds(r0, n_heads), :] = blk_sum * (1.0 / MOBA_BLOCK)
            bmax_scr[j * bps + b] = jnp.broadcast_to(blk_max, (rows, lanes))

    @pl.when(j == ks - 1)
    def _():
        q_pad = jnp.concatenate([q_all, jnp.zeros((lanes - rows, hd), BF16)], axis=0)
        gate_t = _dot_nt(kmean_scr[...].astype(BF16), q_pad)
        tiles = [gate_t[n * n_heads:(n + 1) * n_heads, :] for n in range(n_blocks)]
        sub_i = lax.broadcasted_iota(jnp.int32, (n_heads, lanes), 0)
        own = jnp.where(c_i < rows, jnp.where(sub_i == c_i - n_heads * c_grp, 1.0, 0.0), 0.0)
        keep_tiles = []
        for n in range(n_blocks):
            rank = jnp.zeros((n_heads, lanes), jnp.int32)
            for m in range(n_blocks):
                if m < n:
                    rank = rank + jnp.where(tiles[m] >= tiles[n], 1, 0)
                elif m > n:
                    rank = rank + jnp.where(tiles[m] > tiles[n], 1, 0)
            keep_tiles.append(jnp.where(rank < MOBA_TOPK, own, 0.0))
        keep_tiles.append(jnp.zeros((lanes - n_blocks * n_heads, lanes), F32))
        keep = jnp.concatenate(keep_tiles, axis=0).T
        keep_x = _dot(keep.astype(BF16), expand_ref[...])[0:rows, :]

        pad = jnp.zeros((lanes - rows, hd), F32)
        kn = jnp.concatenate([kn_ref[...], pad], axis=0).astype(BF16)
        vn = jnp.concatenate([vn_ref[...], pad], axis=0).astype(BF16)
        own_ok = jnp.where(c_i < rows, jnp.where(same_head, jnp.where(c_grp <= r_tok, 1, 0), 0), 0)
        s_own = jnp.where(own_ok > 0, _dot_nt(q_all, kn), NEG_INF)

        kept = [keep_x[:, n * lanes:(n + 1) * lanes] > 0.5 for n in range(n_blocks)]
        m_lane = jnp.full((rows, lanes), NEG_INF, F32)
        for n in range(n_blocks):
            m_lane = jnp.maximum(m_lane, jnp.where(kept[n], bmax_scr[n], NEG_INF))
        m_fin = jnp.maximum(jnp.max(s_own, axis=-1, keepdims=True),
                            jnp.max(m_lane, axis=-1, keepdims=True))
        neg_m = jnp.broadcast_to(-m_fin, (rows, lanes))
        for n in range(n_blocks):
            shift_scr[n] = jnp.where(kept[n], neg_m, NEG_INF)
        p_own = jnp.exp2((s_own - m_fin) * c)
        l_scr[...] = jnp.broadcast_to(jnp.sum(p_own, axis=-1, keepdims=True), l_scr.shape)
        acc_scr[...] = _dot(p_own.astype(BF16), vn)

    @pl.when(j >= ks)
    def _():
        acc = acc_scr[...]
        l_add = jnp.zeros((rows, 1), F32)
        for b in range(bps):
            shift = page_wide(shift_scr[(j - ks) * bps + b])
            for e in range(ppb):
                a = b * ppb + e
                pn = jnp.exp2((s_scr[(j - ks) * pps + a] + shift) * c)
                l_add = l_add + jnp.sum(pn, axis=-1, keepdims=True)
                acc = acc + _dot(pn.astype(BF16), ring[slot, a].astype(BF16))
        acc_scr[...] = acc
        l_scr[...] = l_scr[...] + l_add

    @pl.when(j == 2 * ks - 1)
    def _():
        o_ref[...] = acc_scr[...] / l_scr[...]


def _moba_sample(q, k_new, v_new, cache_k, cache_v, page_table, expand):
    n_phys, page, n_heads, hd = cache_k.shape
    db, t_new = q.shape[0], q.shape[1] // n_heads
    cache_k = cache_k.reshape(n_phys, page * n_heads, hd)
    cache_v = cache_v.reshape(n_phys, page * n_heads, hd)
    n_pages = page_table.shape[1]
    pps = SAMPLE_PAGES_PER_STEP
    n_slots = SAMPLE_RING_SLOTS
    assert MOBA_BLOCK % page == 0 and pps % (MOBA_BLOCK // page) == 0 and n_pages % pps == 0
    ks = n_pages // pps
    assert n_slots - 1 <= 2 * ks
    rows = t_new * n_heads
    lanes = V7X_LANES
    assert rows <= lanes and (n_pages * page // MOBA_BLOCK) * n_heads <= lanes and lanes % n_heads == 0
    assert (page * n_heads) % lanes == 0

    seq = lambda s, j, pt: (s, 0, 0)
    new_spec = pl.BlockSpec((None, rows, hd), seq)
    grid_spec = pltpu.PrefetchScalarGridSpec(
        num_scalar_prefetch=1,
        grid=(db, 2 * ks),
        in_specs=[new_spec, new_spec, new_spec, pl.BlockSpec(expand.shape, lambda s, j, pt: (0, 0)),
                  pl.BlockSpec(memory_space=pl.ANY), pl.BlockSpec(memory_space=pl.ANY)],
        out_specs=new_spec,
        scratch_shapes=[
            pltpu.VMEM((lanes, hd), F32),
            pltpu.VMEM((n_pages, rows, page * n_heads), F32),
            pltpu.VMEM((n_pages * page // MOBA_BLOCK, rows, lanes), F32),
            pltpu.VMEM((n_pages * page // MOBA_BLOCK, rows, lanes), F32),
            pltpu.VMEM((rows, hd), F32),
            pltpu.VMEM((rows, hd), F32),
            pltpu.VMEM((n_slots, pps, page * n_heads, hd), F32),
            pltpu.SemaphoreType.DMA((n_slots,)),
        ],
    )
    return pl.pallas_call(
        functools.partial(_moba_sample_kernel, scale=hd ** -0.5, n_pages=n_pages, n_heads=n_heads),
        grid_spec=grid_spec,
        out_shape=jax.ShapeDtypeStruct((db, rows, hd), F32),
        compiler_params=_params(2),
        name="moba_sample",
    )(page_table, q, k_new, v_new, expand, cache_k, cache_v)


def _expand_matrix(n_blocks, n_heads, lanes):
    r = jnp.arange(lanes)[:, None]
    c = jnp.arange(n_blocks * lanes)[None, :]
    return ((r // n_heads == c // lanes) & (r % n_heads == c % n_heads) & (r < n_blocks * n_heads)).astype(BF16)


def _trunk(x, mods, mods_kv, hist, pos0, wts, n_heads, hd, bb, tt, attend, is_prompt):
    t = x.shape[1]
    n_a = wts["w_in_pool"].shape[0]
    n_b = wts["w_in_attn"].shape[0]
    cos, sin = _rope_tables(t, hd, pos0)
    new_hist = []
    for l in range(n_a):
        x, hn = _pool_layer(x, mods[l], hist[l], wts["g_pre"][l], wts["g_post"][l], wts["w_in_pool"][l],
                            wts["w_pool_group"][l], wts["pool_scale"][l], wts["w_out_pool"][l], pos0, bb, tt)
        new_hist.append(hn)
    kv = _kv_proj(x, mods_kv, wts["g_kv"], wts["w_kv"], cos, sin, n_heads, hd, bb, tt, is_prompt)
    for jl in range(n_b):
        l = n_a + jl
        q, gate = _q_proj(x, mods[l], wts["g_pre"][l], wts["w_in_attn"][jl], cos, sin, n_heads, hd, bb, tt,
                          head_split=not is_prompt)
        o = attend(q, kv)
        x = _out_proj(o, gate, x, mods[l], wts["g_post"][l], wts["w_out_attn"][jl], bb, tt,
                      head_split=not is_prompt)
    b = x.shape[0]
    return x, kv[0].reshape(b, t, n_heads, hd), kv[1].reshape(b, t, n_heads, hd), jnp.stack(new_hist)


def kernel(x_prompt, x_sample, cache_k, cache_v, state_pool, page_table, c_prompt, c_sample, w_ada, b_ada,
           g_pre, g_post, w_in_pool, w_pool_group, pool_scale, w_out_pool, g_kv, w_ada_kv, b_ada_kv, w_kv,
           w_in_attn, w_out_attn):
    n_b_p = x_prompt.shape[0]
    page, n_heads, hd = cache_k.shape[1], cache_k.shape[2], cache_k.shape[3]
    past_len = page_table.shape[1] * page
    n_a = w_in_pool.shape[0]
    n_hist = state_pool.shape[2]

    c_all = jnp.concatenate([c_prompt, c_sample], axis=0)
    r = c_all.shape[0]
    r_pad = -(-r // V7X_SUBLANES) * V7X_SUBLANES
    c_all = jnp.pad(c_all, ((0, r_pad - r), (0, 0)))
    mods_all = _ada_mods(c_all, w_ada, b_ada)
    mods_kv_all = _ada_mods(c_all, w_ada_kv[None], b_ada_kv[None])[0]
    mods_p = [mods_all[l, 0:n_b_p][:, None, :] for l in range(mods_all.shape[0])]
    mods_s = [mods_all[l, n_b_p:r][:, None, :] for l in range(mods_all.shape[0])]
    mods_kv_p = mods_kv_all[0:n_b_p][:, None, :]
    mods_kv_s = mods_kv_all[n_b_p:r][:, None, :]

    wts = dict(
        g_pre=g_pre, g_post=g_post, pool_scale=pool_scale, g_kv=g_kv,
        w_in_pool=w_in_pool.astype(BF16), w_pool_group=w_pool_group.astype(BF16),
        w_out_pool=w_out_pool.astype(BF16), w_kv=w_kv.astype(BF16),
        w_in_attn=w_in_attn.astype(BF16), w_out_attn=w_out_attn.astype(BF16))

    def attend_prompt(q, kv):
        _, _, kb, vt, kmean = kv
        kmean = kmean.reshape(kmean.shape[0], -1, kmean.shape[-1])
        return _moba_prompt(q, kb, vt, kmean, n_heads, hd, g_heads=PROMPT_HEADS_PER_STEP)

    expand = _expand_matrix(past_len // MOBA_BLOCK, n_heads, V7X_LANES)

    def attend_sample(q, kv):
        return _moba_sample(q, kv[0], kv[1], cache_k, cache_v, page_table, expand)

    hist0 = jnp.zeros((n_a, n_b_p, n_hist, state_pool.shape[3]), F32)
    y_p, k_p, v_p, pool_p = _trunk(x_prompt, mods_p, mods_kv_p, hist0, 0, wts, n_heads, hd,
                                   bb=1, tt=PROMPT_TOKENS_PER_STEP, attend=attend_prompt, is_prompt=True)
    y_s, k_s, v_s, pool_s = _trunk(x_sample, mods_s, mods_kv_s, state_pool, past_len, wts, n_heads, hd,
                                   bb=SAMPLE_SEQS_PER_STEP, tt=x_sample.shape[1], attend=attend_sample,
                                   is_prompt=False)
    return (y_p, y_s, k_p, v_p, k_s, v_s, pool_p, pool_s)
```

```python
import functools
import math

import jax
import jax.numpy as jnp
from jax import lax
from jax.experimental import pallas as pl
from jax.experimental.pallas import tpu as pltpu

F32 = jnp.float32
BF16 = jnp.bfloat16
NEG_INF = float("-inf")
LOG2E = math.log2(math.e)

POOL_WINDOWS = (2, 4, 8, 16)
MOBA_BLOCK = 256
MOBA_TOPK = 3
ROPE_THETA = 10000.0
NORM_EPS = 1e-6

V7X_LANES = 128
V7X_SUBLANES = 8
V7X_VMEM_LIMIT_BYTES = 56 * 1024 * 1024

HIST_PAD = 16
SUM_ROWS = 16
PROMPT_HEADS_PER_STEP = 8
PROMPT_BLOCKS_PER_ITER = 8
SAMPLE_PAGES_PER_STEP = 16
SAMPLE_RING_SLOTS = 3
PROMPT_POOL_TOKENS_PER_STEP = 512
PROMPT_PROJ_TOKENS_PER_STEP = 1024
SAMPLE_SEQS_PER_STEP = 64


def _params(n_axes):
    return pltpu.CompilerParams(
        dimension_semantics=("arbitrary",) * n_axes, vmem_limit_bytes=V7X_VMEM_LIMIT_BYTES)


def _silu(x):
    return x * (1.0 / (1.0 + jnp.exp(-x)))


def _rms(x, g):
    return x * lax.rsqrt(jnp.mean(x * x, axis=-1, keepdims=True) + NORM_EPS) * g


def _dot(a, b):
    return jnp.dot(a, b, preferred_element_type=F32)


def _bucket(idx, size, n):
    out = jnp.zeros_like(idx)
    for h in range(1, n):
        out = out + jnp.where(idx >= h * size, 1, 0)
    return out


def _dot_nt(a, b):
    return lax.dot_general(a, b, (((1,), (1,)), ((), ())), preferred_element_type=F32)


def _rope_table_kernel(cos_ref, sin_ref, *, pos0):
    t, hd = cos_ref.shape
    half = hd // 2
    lane = lax.broadcasted_iota(jnp.int32, (t, hd), 1)
    row = lax.broadcasted_iota(jnp.int32, (t, hd), 0)
    j = jnp.where(lane < half, lane, lane - half).astype(F32)
    inv = jnp.exp(j * (-2.0 * math.log(ROPE_THETA) / hd))
    ang = (row + pos0).astype(F32) * inv
    cos_ref[...] = jnp.cos(ang)
    s = jnp.sin(ang)
    sin_ref[...] = jnp.where(lane < half, -s, s)


def _rope_tables(t, hd, pos0):
    return pl.pallas_call(
        functools.partial(_rope_table_kernel, pos0=pos0),
        out_shape=(jax.ShapeDtypeStruct((t, hd), F32), jax.ShapeDtypeStruct((t, hd), F32)),
        name="rope_tables",
    )()


def _head_rows(ref, h, n_tok, n_heads):
    return (slice(None),) * (len(ref.shape) - 2) + (pl.ds(h, n_tok, stride=n_heads), slice(None))


def _rope_head(a, h, cos, sin, bb, tt, hd):
    slab = a[:, h * hd:(h + 1) * hd]
    rot = pltpu.roll(slab, hd // 2, 1)
    return slab.reshape(bb, tt, hd) * cos[None] + rot.reshape(bb, tt, hd) * sin[None]


def _ada_kernel(c_ref, w_ref, b_ref, o_ref):
    h = _silu(c_ref[...]).astype(BF16)
    o_ref[...] = _dot(h, w_ref[...].astype(BF16)) + b_ref[...]


def _ada_mods(c, w, b):
    n_l, d, n = w.shape
    r = c.shape[0]
    bn = 1024
    return pl.pallas_call(
        _ada_kernel,
        grid=(n_l, n // bn),
        in_specs=[
            pl.BlockSpec((r, d), lambda l, j: (0, 0)),
            pl.BlockSpec((None, d, bn), lambda l, j: (l, 0, j)),
            pl.BlockSpec((None, 1, bn), lambda l, j: (l, 0, j)),
        ],
        out_specs=pl.BlockSpec((None, r, bn), lambda l, j: (l, 0, j)),
        out_shape=jax.ShapeDtypeStruct((n_l, r, n), F32),
        compiler_params=_params(2),
        name="ada_mods",
    )(c, w, b.reshape(n_l, 1, n))


def _pool_kernel(x_ref, mod_ref, hist_ref, gpre_ref, gpost_ref, win_ref, wgrp_ref, pscale_ref, wout_ref,
                 y_ref, hist_out_ref, z_scr, *, pos0):
    t = pl.program_id(1)
    n_t = pl.num_programs(1)
    bb, tt, d = x_ref.shape
    w = z_scr.shape[-1]
    n_hist = hist_ref.shape[1]
    m = bb * tt
    grp = w // len(POOL_WINDOWS)

    @pl.when(t == 0)
    def _():
        z_scr[:, HIST_PAD - n_hist:HIST_PAD, :] = hist_ref[...]

    @pl.when(t > 0)
    def _():
        z_scr[:, 0:HIST_PAD, :] = z_scr[:, tt:tt + HIST_PAD, :]

    mod = mod_ref[...]
    shift, scale, gate = mod[:, :, 0:d], mod[:, :, d:2 * d], mod[:, :, 2 * d:3 * d]
    x = x_ref[...]
    h = _rms(x, gpre_ref[...]) * (1.0 + scale) + shift
    ug = _dot(h.reshape(m, d).astype(BF16), win_ref[...])
    u = ug[:, 0:w]
    z_scr[:, HIST_PAD:HIST_PAD + tt, :] = u.reshape(bb, tt, w)

    pos = pos0 + t * tt + lax.broadcasted_iota(jnp.int32, (1, tt, 1), 1)
    ys = []
    for g, win in enumerate(POOL_WINDOWS):
        sl = slice(g * grp, (g + 1) * grp)
        acc = z_scr[:, HIST_PAD:HIST_PAD + tt, sl]
        for k in range(1, win):
            acc = acc + z_scr[:, HIST_PAD - k:HIST_PAD - k + tt, sl]
        inv_cnt = 1.0 / jnp.minimum(pos + 1, win).astype(F32)
        dm = acc * inv_cnt - z_scr[:, HIST_PAD:HIST_PAD + tt, sl]
        ys.append(_dot(dm.reshape(m, grp).astype(BF16), wgrp_ref[g]))
    y = jnp.concatenate(ys, axis=-1) * pscale_ref[...]
    a = (y * _silu(ug[:, w:2 * w])).astype(BF16)
    o = _dot(a, wout_ref[...]).reshape(bb, tt, d)
    y_ref[...] = x + gate * _rms(o, gpost_ref[...])

    @pl.when(t == n_t - 1)
    def _():
        hist_out_ref[...] = z_scr[:, HIST_PAD + tt - n_hist:HIST_PAD + tt, :]


def _pool_layer(x, mod, hist, g_pre, g_post, w_in, w_grp, pool_scale, w_out, pos0, bb, tt):
    b, t, d = x.shape
    w = w_out.shape[0]
    n_hist = hist.shape[1]
    n_g, grp, _ = w_grp.shape
    const2 = lambda i, j: (0, 0)
    return pl.pallas_call(
        functools.partial(_pool_kernel, pos0=pos0),
        grid=(b // bb, t // tt),
        in_specs=[
            pl.BlockSpec((bb, tt, d), lambda i, j: (i, j, 0)),
            pl.BlockSpec((bb, 1, 3 * d), lambda i, j: (i, 0, 0)),
            pl.BlockSpec((bb, n_hist, w), lambda i, j: (i, 0, 0)),
            pl.BlockSpec((1, d), const2),
            pl.BlockSpec((1, d), const2),
            pl.BlockSpec((d, 2 * w), const2),
            pl.BlockSpec((n_g, grp, grp), lambda i, j: (0, 0, 0)),
            pl.BlockSpec((1, w), const2),
            pl.BlockSpec((w, d), const2),
        ],
        out_specs=(
            pl.BlockSpec((bb, tt, d), lambda i, j: (i, j, 0)),
            pl.BlockSpec((bb, n_hist, w), lambda i, j: (i, 0, 0)),
        ),
        out_shape=(jax.ShapeDtypeStruct((b, t, d), F32), jax.ShapeDtypeStruct((b, n_hist, w), F32)),
        scratch_shapes=[pltpu.VMEM((bb, HIST_PAD + tt, w), F32)],
        compiler_params=_params(2),
        name="pool_layer",
    )(x, mod, hist, g_pre.reshape(1, d), g_post.reshape(1, d), w_in, w_grp, pool_scale.reshape(1, w), w_out)


def _normed(x_ref, mod_ref, g_ref):
    bb, tt, d = x_ref.shape
    mod = mod_ref[...]
    shift, scale = mod[:, :, 0:d], mod[:, :, d:2 * d]
    h = _rms(x_ref[...], g_ref[...]) * (1.0 + scale) + shift
    return h.reshape(bb * tt, d).astype(BF16)


def _kv_kernel(x_ref, mod_ref, g_ref, w_ref, cos_ref, sin_ref, k_ref, v_ref, *extra, n_heads, hd):
    bb, tt, _ = x_ref.shape
    kvw = n_heads * hd
    kv = _dot(_normed(x_ref, mod_ref, g_ref), w_ref[...])
    cos, sin = cos_ref[...], sin_ref[...]
    n_blk = tt // MOBA_BLOCK
    for h in range(n_heads):
        k_h = _rope_head(kv, h, cos, sin, bb, tt, hd)
        v_h = kv[:, kvw + h * hd:kvw + (h + 1) * hd]
        k_ref[_head_rows(k_ref, h, tt, n_heads)] = k_h
        v_ref[_head_rows(v_ref, h, tt, n_heads)] = v_h.reshape(bb, tt, hd)
        if extra:
            kb_ref, vt_ref, kmean_ref = extra
            kb_ref[:, :, h * hd:(h + 1) * hd] = k_h.astype(BF16)
            kmean_ref[:, h * hd:(h + 1) * hd] = jnp.mean(k_h.reshape(n_blk, MOBA_BLOCK, hd), axis=1)
            for j in range(n_blk):
                vt_ref[h, j] = v_h[j * MOBA_BLOCK:(j + 1) * MOBA_BLOCK, :].T.astype(BF16)


def _kv_proj(x, mod, g, w, cos, sin, n_heads, hd, bb, tt, with_attn_layouts):
    b, t, d = x.shape
    kvw = n_heads * hd
    const2 = lambda i, j: (0, 0)
    tile = lambda i, j: (i, j, 0)
    out_specs = [pl.BlockSpec((bb, tt * n_heads, hd), tile), pl.BlockSpec((bb, tt * n_heads, hd), tile)]
    out_shape = [jax.ShapeDtypeStruct((b, t * n_heads, hd), F32), jax.ShapeDtypeStruct((b, t * n_heads, hd), F32)]
    if with_attn_layouts:
        assert bb == 1 and tt % MOBA_BLOCK == 0
        n_blk = tt // MOBA_BLOCK
        out_specs += [
            pl.BlockSpec((bb, tt, kvw), tile),
            pl.BlockSpec((None, n_heads, n_blk, hd, MOBA_BLOCK), lambda i, j: (i, 0, j, 0, 0)),
            pl.BlockSpec((None, None, n_blk, kvw), lambda i, j: (i, j, 0, 0)),
        ]
        out_shape += [
            jax.ShapeDtypeStruct((b, t, kvw), BF16),
            jax.ShapeDtypeStruct((b, n_heads, t // MOBA_BLOCK, hd, MOBA_BLOCK), BF16),
            jax.ShapeDtypeStruct((b, t // tt, n_blk, kvw), F32),
        ]
    return pl.pallas_call(
        functools.partial(_kv_kernel, n_heads=n_heads, hd=hd),
        grid=(b // bb, t // tt),
        in_specs=[
            pl.BlockSpec((bb, tt, d), tile),
            pl.BlockSpec((bb, 1, mod.shape[-1]), lambda i, j: (i, 0, 0)),
            pl.BlockSpec((1, d), const2),
            pl.BlockSpec((d, 2 * kvw), const2),
            pl.BlockSpec((tt, hd), lambda i, j: (j, 0)),
            pl.BlockSpec((tt, hd), lambda i, j: (j, 0)),
        ],
        out_specs=tuple(out_specs),
        out_shape=tuple(out_shape),
        compiler_params=_params(2),
        name="kv_proj",
    )(x, mod, g.reshape(1, d), w, cos, sin)


def _q_kernel(x_ref, mod_ref, g_ref, w_ref, cos_ref, sin_ref, q_ref, gate_ref, *, n_heads, hd, head_split):
    bb, tt, _ = x_ref.shape
    aw = n_heads * hd
    qg = _dot(_normed(x_ref, mod_ref, g_ref), w_ref[...])
    cos, sin = cos_ref[...], sin_ref[...]
    for h in range(n_heads):
        q_h = _rope_head(qg, h, cos, sin, bb, tt, hd)
        if head_split:
            q_ref[_head_rows(q_ref, h, tt, n_heads)] = q_h
        else:
            q_ref[:, :, h * hd:(h + 1) * hd] = q_h.astype(BF16)
    gate_ref[...] = qg[:, aw:2 * aw].reshape(bb, tt, aw).astype(gate_ref.dtype)


def _q_proj(x, mod, g, w, cos, sin, n_heads, hd, bb, tt, head_split):
    b, t, d = x.shape
    aw = n_heads * hd
    const2 = lambda i, j: (0, 0)
    tile = lambda i, j: (i, j, 0)
    if head_split:
        q_spec = pl.BlockSpec((bb, tt * n_heads, hd), tile)
        q_shape = jax.ShapeDtypeStruct((b, t * n_heads, hd), F32)
    else:
        q_spec = pl.BlockSpec((bb, tt, aw), tile)
        q_shape = jax.ShapeDtypeStruct((b, t, aw), BF16)
    return pl.pallas_call(
        functools.partial(_q_kernel, n_heads=n_heads, hd=hd, head_split=head_split),
        grid=(b // bb, t // tt),
        in_specs=[
            pl.BlockSpec((bb, tt, d), tile),
            pl.BlockSpec((bb, 1, mod.shape[-1]), lambda i, j: (i, 0, 0)),
            pl.BlockSpec((1, d), const2),
            pl.BlockSpec((d, 2 * aw), const2),
            pl.BlockSpec((tt, hd), lambda i, j: (j, 0)),
            pl.BlockSpec((tt, hd), lambda i, j: (j, 0)),
        ],
        out_specs=(q_spec, pl.BlockSpec((bb, tt, aw), tile)),
        out_shape=(q_shape, jax.ShapeDtypeStruct((b, t, aw), q_shape.dtype)),
        compiler_params=_params(2),
        name="q_proj",
    )(x, mod, g.reshape(1, d), w, cos, sin)


def _out_kernel(o_ref, gate_ref, x_ref, mod_ref, gpost_ref, w_ref, y_ref, *, head_split):
    bb, tt, d = x_ref.shape
    aw = gate_ref.shape[-1]
    sg = _silu(gate_ref[...].astype(F32))
    if head_split:
        hd = o_ref.shape[2]
        n_heads = aw // hd
        r = jnp.zeros((bb * tt, d), F32)
        for h in range(n_heads):
            o_h = o_ref[_head_rows(o_ref, h, tt, n_heads)]
            a_h = (o_h * sg[:, :, h * hd:(h + 1) * hd]).reshape(bb * tt, hd).astype(BF16)
            r = r + _dot(a_h, w_ref[h * hd:(h + 1) * hd, :])
    else:
        a = (o_ref[...].astype(F32) * sg).reshape(bb * tt, aw).astype(BF16)
        r = _dot(a, w_ref[...])
    res_gate = mod_ref[...][:, :, 2 * d:3 * d]
    y_ref[...] = x_ref[...] + res_gate * _rms(r.reshape(bb, tt, d), gpost_ref[...])


def _out_proj(o, gate, x, mod, g_post, w, bb, tt, head_split):
    b, t, d = x.shape
    aw = gate.shape[-1]
    tile = lambda i, j: (i, j, 0)
    if head_split:
        o_spec = pl.BlockSpec((bb, o.shape[1] // t * tt, o.shape[2]), tile)
    else:
        o_spec = pl.BlockSpec((bb, tt, aw), tile)
    return pl.pallas_call(
        functools.partial(_out_kernel, head_split=head_split),
        grid=(b // bb, t // tt),
        in_specs=[
            o_spec,
            pl.BlockSpec((bb, tt, aw), tile),
            pl.BlockSpec((bb, tt, d), tile),
            pl.BlockSpec((bb, 1, mod.shape[-1]), lambda i, j: (i, 0, 0)),
            pl.BlockSpec((1, d), lambda i, j: (0, 0)),
            pl.BlockSpec((aw, d), lambda i, j: (0, 0)),
        ],
        out_specs=pl.BlockSpec((bb, tt, d), tile),
        out_shape=jax.ShapeDtypeStruct((b, t, d), F32),
        compiler_params=_params(2),
        name="out_proj",
    )(o, gate, x, mod, g_post.reshape(1, d), w)


def _select_blocks_t(gate_t, n_past):
    nb = gate_t.shape[0]
    blk = lax.broadcasted_iota(jnp.int32, gate_t.shape, 0)
    g = jnp.where(blk < n_past, gate_t, NEG_INF)
    rank = jnp.zeros(gate_t.shape, jnp.int32)
    for m in range(nb):
        gm = g[m:m + 1, :]
        rank = rank + jnp.where(gm > g, 1, jnp.where(gm == g, jnp.where(m < blk, 1, 0), 0))
    return jnp.where(blk < n_past, jnp.where(rank < MOBA_TOPK, 1, 0), 0)


def _moba_prompt_kernel(q_ref, k_ref, vt_ref, kmean_ref, o_ref, shift_scr, s_scr, *, scale, hd):
    i = pl.program_id(2)
    g_heads = q_ref.shape[1] // hd
    c = scale * LOG2E
    ones_rows = jnp.ones((SUM_ROWS, MOBA_BLOCK), BF16)

    def values_and_sums(g, n, p):
        return _dot(jnp.concatenate([vt_ref[g, n], ones_rows], axis=0), p.astype(BF16))

    start = pl.multiple_of(i * MOBA_BLOCK, MOBA_BLOCK)
    key_i = lax.broadcasted_iota(jnp.int32, (MOBA_BLOCK, MOBA_BLOCK), 0)
    qry_i = lax.broadcasted_iota(jnp.int32, (MOBA_BLOCK, MOBA_BLOCK), 1)
    causal = key_i <= qry_i

    q_ts, s_owns, m0s = [], [], []
    for g in range(g_heads):
        ln = slice(g * hd, (g + 1) * hd)
        q_t = q_ref[:, ln].astype(F32).T.astype(BF16)
        gate_t = _dot(kmean_ref[:, ln].astype(BF16), q_t)
        keep = _select_blocks_t(gate_t, i)
        shift_scr[g] = jnp.where(keep > 0, 0.0, NEG_INF)
        s = jnp.where(causal, _dot(k_ref[pl.ds(start, MOBA_BLOCK), ln], q_t) * c, NEG_INF)
        q_ts.append(q_t)
        s_owns.append(s)
        m0s.append(jnp.max(s, axis=0, keepdims=True))

    per_iter = PROMPT_BLOCKS_PER_ITER
    n_full = i // per_iter
    n_rem = i - n_full * per_iter

    def over_past_blocks(blocks_fn, carry):
        carry = lax.fori_loop(0, n_full, lambda t, cr: blocks_fn(t * per_iter, per_iter, cr), carry)
        done = n_full * per_iter
        size = per_iter // 2
        while size >= 1:
            take = (n_rem & size) != 0
            carry = lax.cond(take, lambda cr, done=done, size=size: blocks_fn(done, size, cr), lambda cr: cr, carry)
            done = done + jnp.where(take, size, 0)
            size //= 2
        return carry

    def scores_blocks(heads, n0, count, ms):
        out = list(ms)
        for u in range(count):
            n = n0 + u
            st = pl.multiple_of(n * MOBA_BLOCK, MOBA_BLOCK)
            for k, g in enumerate(heads):
                sn = _dot(k_ref[pl.ds(st, MOBA_BLOCK), g * hd:(g + 1) * hd], q_ts[g]) * c
                s_scr[g, n] = sn
                out[k] = jnp.maximum(out[k], jnp.max(sn, axis=0, keepdims=True) + shift_scr[g, pl.ds(n, 1), :])
        return tuple(out)

    def values_blocks(heads, ms, n0, count, accs):
        out = list(accs)
        for u in range(count):
            n = n0 + u
            for k, g in enumerate(heads):
                pn = jnp.exp2(s_scr[g, n] - (ms[k] - shift_scr[g, pl.ds(n, 1), :]))
                out[k] = out[k] + values_and_sums(g, n, pn)
        return tuple(out)

    def own_values(heads, ms):
        return tuple(values_and_sums(g, i, jnp.exp2(s_owns[g] - ms[k])) for k, g in enumerate(heads))

    heads = tuple(range(g_heads))
    ms = over_past_blocks(lambda n0, cnt, cr: scores_blocks(heads, n0, cnt, cr), tuple(m0s))
    accs = over_past_blocks(lambda n0, cnt, cr: values_blocks(heads, ms, n0, cnt, cr), own_values(heads, ms))
    for g in heads:
        o_ref[:, g * hd:(g + 1) * hd] = (accs[g][0:hd, :] / accs[g][hd:hd + 1, :]).T.astype(o_ref.dtype)


def _moba_prompt(q, kb, vt, kmean, n_heads, hd, g_heads):
    b, s, aw = q.shape
    n_blk = s // MOBA_BLOCK
    gw = g_heads * hd
    return pl.pallas_call(
        functools.partial(_moba_prompt_kernel, scale=hd ** -0.5, hd=hd),
        grid=(b, n_heads // g_heads, n_blk),
        in_specs=[
            pl.BlockSpec((None, MOBA_BLOCK, gw), lambda bi, h, i: (bi, i, h)),
            pl.BlockSpec((None, s, gw), lambda bi, h, i: (bi, 0, h), pipeline_mode=pl.Buffered(1)),
            pl.BlockSpec((None, g_heads, n_blk, hd, MOBA_BLOCK), lambda bi, h, i: (bi, h, 0, 0, 0),
                         pipeline_mode=pl.Buffered(1)),
            pl.BlockSpec((None, n_blk, gw), lambda bi, h, i: (bi, 0, h)),
        ],
        out_specs=pl.BlockSpec((None, MOBA_BLOCK, gw), lambda bi, h, i: (bi, i, h)),
        out_shape=jax.ShapeDtypeStruct((b, s, aw), BF16),
        scratch_shapes=[
            pltpu.VMEM((g_heads, n_blk, MOBA_BLOCK), F32),
            pltpu.VMEM((g_heads, n_blk, MOBA_BLOCK, MOBA_BLOCK), F32),
        ],
        compiler_params=_params(3),
        name="moba_prompt",
    )(q, kb, vt, kmean)


def _moba_sample_kernel(pt_ref, q_ref, kn_ref, vn_ref, expand_ref, ck_ref, cv_ref, o_ref,
                        kmean_scr, s_scr, bmax_scr, shift_scr, l_scr, acc_scr, ring, sems,
                        *, scale, n_pages, n_heads):
    seq = pl.program_id(0)
    j = pl.program_id(1)
    n_slots, pps, pr, hd = ring.shape
    ks = n_pages // pps
    steps = 2 * ks
    look = n_slots - 1
    rows = q_ref.shape[0]
    t_new = rows // n_heads
    page = pr // n_heads

    def page_copy(cache_ref, page_id, slot, a):
        return pltpu.make_async_copy(cache_ref.at[page_id], ring.at[slot, a], sems.at[slot])

    def start_step(seq_i, j_i, slot):
        is_k = j_i < ks
        first = jnp.where(is_k, j_i, j_i - ks) * pps
        for a in range(pps):
            page_id = pt_ref[seq_i, first + a]

            @pl.when(is_k)
            def _():
                page_copy(ck_ref, page_id, slot, a).start()

            @pl.when(jnp.logical_not(is_k))
            def _():
                page_copy(cv_ref, page_id, slot, a).start()

    @pl.when(jnp.logical_and(seq == 0, j == 0))
    def _():
        for ahead in range(look):
            start_step(0, ahead, ahead)

    g = seq * steps + j
    j_next = j + look
    wrap = j_next >= steps
    seq_next = jnp.where(wrap, seq + 1, seq)

    @pl.when(seq_next < pl.num_programs(0))
    def _():
        start_step(seq_next, jnp.where(wrap, j_next - steps, j_next), lax.rem(g + look, n_slots))

    slot = lax.rem(g, n_slots)
    for a in range(pps):
        page_copy(ck_ref, 0, slot, a).wait()
    ppb = MOBA_BLOCK // page
    n_blocks = n_pages // ppb
    bps = pps // ppb
    lanes = kmean_scr.shape[0]
    c = scale * LOG2E
    q_all = q_ref[...].astype(BF16)

    r_i = lax.broadcasted_iota(jnp.int32, (rows, 1), 0)
    c_i = lax.broadcasted_iota(jnp.int32, (1, lanes), 1)
    r_tok = _bucket(r_i, n_heads, t_new)
    c_grp = _bucket(c_i, n_heads, lanes // n_heads)
    same_head = (r_i - n_heads * r_tok) == (c_i - n_heads * c_grp)

    def page_wide(tile):
        return jnp.concatenate([tile] * (pr // lanes), axis=1)

    @pl.when(j == 0)
    def _():
        kmean_scr[...] = jnp.zeros_like(kmean_scr)

    @pl.when(j < ks)
    def _():
        head_bias = page_wide(jnp.where(same_head, 0.0, NEG_INF))
        sums, maxes = [], []
        for a in range(pps):
            kp = ring[slot, a]
            sums.append(jnp.sum(kp.reshape(page, n_heads, hd), axis=0))
            s = _dot_nt(q_all, kp.astype(BF16))
            s_scr[j * pps + a] = s
            maxes.append(jnp.max(s + head_bias, axis=-1, keepdims=True))
        for b in range(bps):
            blk_sum, blk_max = sums[b * ppb], maxes[b * ppb]
            for e in range(1, ppb):
                blk_sum = blk_sum + sums[b * ppb + e]
                blk_max = jnp.maximum(blk_max, maxes[b * ppb + e])
            r0 = pl.multiple_of((j * bps + b) * n_heads, n_heads)
            kmean_scr[pl.ds(r0, n_heads), :] = blk_sum * (1.0 / MOBA_BLOCK)
            bmax_scr[j * bps + b] = jnp.broadcast_to(blk_max, (rows, lanes))

    @pl.when(j == ks - 1)
    def _():
        q_pad = jnp.concatenate([q_all, jnp.zeros((lanes - rows, hd), BF16)], axis=0)
        gate_t = _dot_nt(kmean_scr[...].astype(BF16), q_pad)
        tiles = [gate_t[n * n_heads:(n + 1) * n_heads, :] for n in range(n_blocks)]
        sub_i = lax.broadcasted_iota(jnp.int32, (n_heads, lanes), 0)
        own = jnp.where(c_i < rows, jnp.where(sub_i == c_i - n_heads * c_grp, 1.0, 0.0), 0.0)
        keep_tiles = []
        for n in range(n_blocks):
            rank = jnp.zeros((n_heads, lanes), jnp.int32)
            for m in range(n_blocks):
                if m < n:
                    rank = rank + jnp.where(tiles[m] >= tiles[n], 1, 0)
                elif m > n:
                    rank = rank + jnp.where(tiles[m] > tiles[n], 1, 0)
            keep_tiles.append(jnp.where(rank < MOBA_TOPK, own, 0.0))
        keep_tiles.append(jnp.zeros((lanes - n_blocks * n_heads, lanes), F32))
        keep = jnp.concatenate(keep_tiles, axis=0).T
        keep_x = _dot(keep.astype(BF16), expand_ref[...])[0:rows, :]

        pad = jnp.zeros((lanes - rows, hd), F32)
        kn = jnp.concatenate([kn_ref[...], pad], axis=0).astype(BF16)
        vn = jnp.concatenate([vn_ref[...], pad], axis=0).astype(BF16)
        own_ok = jnp.where(c_i < rows, jnp.where(same_head, jnp.where(c_grp <= r_tok, 1, 0), 0), 0)
        s_own = jnp.where(own_ok > 0, _dot_nt(q_all, kn), NEG_INF)

        kept = [keep_x[:, n * lanes:(n + 1) * lanes] > 0.5 for n in range(n_blocks)]
        m_lane = jnp.full((rows, lanes), NEG_INF, F32)
        for n in range(n_blocks):
            m_lane = jnp.maximum(m_lane, jnp.where(kept[n], bmax_scr[n], NEG_INF))
        m_fin = jnp.maximum(jnp.max(s_own, axis=-1, keepdims=True),
                            jnp.max(m_lane, axis=-1, keepdims=True))
        neg_m = jnp.broadcast_to(-m_fin, (rows, lanes))
        for n in range(n_blocks):
            shift_scr[n] = jnp.where(kept[n], neg_m, NEG_INF)
        p_own = jnp.exp2((s_own - m_fin) * c)
        l_scr[...] = jnp.broadcast_to(jnp.sum(p_own, axis=-1, keepdims=True), l_scr.shape)
        acc_scr[...] = _dot(p_own.astype(BF16), vn)

    @pl.when(j >= ks)
    def _():
        acc = acc_scr[...]
        l_add = jnp.zeros((rows, 1), F32)
        for b in range(bps):
            shift = page_wide(shift_scr[(j - ks) * bps + b])
            for e in range(ppb):
                a = b * ppb + e
                pn = jnp.exp2((s_scr[(j - ks) * pps + a] + shift) * c)
                l_add = l_add + jnp.sum(pn, axis=-1, keepdims=True)
                acc = acc + _dot(pn.astype(BF16), ring[slot, a].astype(BF16))
        acc_scr[...] = acc
        l_scr[...] = l_scr[...] + l_add

    @pl.when(j == 2 * ks - 1)
    def _():
        o_ref[...] = acc_scr[...] / l_scr[...]


def _moba_sample(q, k_new, v_new, cache_k, cache_v, page_table, expand):
    n_phys, page, n_heads, hd = cache_k.shape
    db, t_new = q.shape[0], q.shape[1] // n_heads
    cache_k = cache_k.reshape(n_phys, page * n_heads, hd)
    cache_v = cache_v.reshape(n_phys, page * n_heads, hd)
    n_pages = page_table.shape[1]
    pps = SAMPLE_PAGES_PER_STEP
    n_slots = SAMPLE_RING_SLOTS
    assert MOBA_BLOCK % page == 0 and pps % (MOBA_BLOCK // page) == 0 and n_pages % pps == 0
    ks = n_pages // pps
    assert n_slots - 1 <= 2 * ks
    rows = t_new * n_heads
    lanes = V7X_LANES
    assert rows <= lanes and (n_pages * page // MOBA_BLOCK) * n_heads <= lanes and lanes % n_heads == 0
    assert (page * n_heads) % lanes == 0

    seq = lambda s, j, pt: (s, 0, 0)
    new_spec = pl.BlockSpec((None, rows, hd), seq)
    grid_spec = pltpu.PrefetchScalarGridSpec(
        num_scalar_prefetch=1,
        grid=(db, 2 * ks),
        in_specs=[new_spec, new_spec, new_spec, pl.BlockSpec(expand.shape, lambda s, j, pt: (0, 0)),
                  pl.BlockSpec(memory_space=pl.ANY), pl.BlockSpec(memory_space=pl.ANY)],
        out_specs=new_spec,
        scratch_shapes=[
            pltpu.VMEM((lanes, hd), F32),
            pltpu.VMEM((n_pages, rows, page * n_heads), F32),
            pltpu.VMEM((n_pages * page // MOBA_BLOCK, rows, lanes), F32),
            pltpu.VMEM((n_pages * page // MOBA_BLOCK, rows, lanes), F32),
            pltpu.VMEM((rows, hd), F32),
            pltpu.VMEM((rows, hd), F32),
            pltpu.VMEM((n_slots, pps, page * n_heads, hd), F32),
            pltpu.SemaphoreType.DMA((n_slots,)),
        ],
    )
    return pl.pallas_call(
        functools.partial(_moba_sample_kernel, scale=hd ** -0.5, n_pages=n_pages, n_heads=n_heads),
        grid_spec=grid_spec,
        out_shape=jax.ShapeDtypeStruct((db, rows, hd), F32),
        compiler_params=_params(2),
        name="moba_sample",
    )(page_table, q, k_new, v_new, expand, cache_k, cache_v)


def _expand_matrix(n_blocks, n_heads, lanes):
    r = jnp.arange(lanes)[:, None]
    c = jnp.arange(n_blocks * lanes)[None, :]
    return ((r // n_heads == c // lanes) & (r % n_heads == c % n_heads) & (r < n_blocks * n_heads)).astype(BF16)


def _trunk(x, mods, mods_kv, hist, pos0, wts, n_heads, hd, bb, tt, tt_proj, attend, is_prompt):
    t = x.shape[1]
    n_a = wts["w_in_pool"].shape[0]
    n_b = wts["w_in_attn"].shape[0]
    cos, sin = _rope_tables(t, hd, pos0)
    new_hist = []
    for l in range(n_a):
        x, hn = _pool_layer(x, mods[l], hist[l], wts["g_pre"][l], wts["g_post"][l], wts["w_in_pool"][l],
                            wts["w_pool_group"][l], wts["pool_scale"][l], wts["w_out_pool"][l], pos0, bb, tt)
        new_hist.append(hn)
    kv = _kv_proj(x, mods_kv, wts["g_kv"], wts["w_kv"], cos, sin, n_heads, hd, bb, tt_proj, is_prompt)
    for jl in range(n_b):
        l = n_a + jl
        q, gate = _q_proj(x, mods[l], wts["g_pre"][l], wts["w_in_attn"][jl], cos, sin, n_heads, hd, bb, tt_proj,
                          head_split=not is_prompt)
        o = attend(q, kv)
        x = _out_proj(o, gate, x, mods[l], wts["g_post"][l], wts["w_out_attn"][jl], bb, tt_proj,
                      head_split=not is_prompt)
    b = x.shape[0]
    return x, kv[0].reshape(b, t, n_heads, hd), kv[1].reshape(b, t, n_heads, hd), jnp.stack(new_hist)


def kernel(x_prompt, x_sample, cache_k, cache_v, state_pool, page_table, c_prompt, c_sample, w_ada, b_ada,
           g_pre, g_post, w_in_pool, w_pool_group, pool_scale, w_out_pool, g_kv, w_ada_kv, b_ada_kv, w_kv,
           w_in_attn, w_out_attn):
    n_b_p = x_prompt.shape[0]
    page, n_heads, hd = cache_k.shape[1], cache_k.shape[2], cache_k.shape[3]
    past_len = page_table.shape[1] * page
    n_a = w_in_pool.shape[0]
    n_hist = state_pool.shape[2]

    c_all = jnp.concatenate([c_prompt, c_sample], axis=0)
    r = c_all.shape[0]
    r_pad = -(-r // V7X_SUBLANES) * V7X_SUBLANES
    c_all = jnp.pad(c_all, ((0, r_pad - r), (0, 0)))
    mods_all = _ada_mods(c_all, w_ada, b_ada)
    mods_kv_all = _ada_mods(c_all, w_ada_kv[None], b_ada_kv[None])[0]
    mods_p = [mods_all[l, 0:n_b_p][:, None, :] for l in range(mods_all.shape[0])]
    mods_s = [mods_all[l, n_b_p:r][:, None, :] for l in range(mods_all.shape[0])]
    mods_kv_p = mods_kv_all[0:n_b_p][:, None, :]
    mods_kv_s = mods_kv_all[n_b_p:r][:, None, :]

    wts = dict(
        g_pre=g_pre, g_post=g_post, pool_scale=pool_scale, g_kv=g_kv,
        w_in_pool=w_in_pool.astype(BF16), w_pool_group=w_pool_group.astype(BF16),
        w_out_pool=w_out_pool.astype(BF16), w_kv=w_kv.astype(BF16),
        w_in_attn=w_in_attn.astype(BF16), w_out_attn=w_out_attn.astype(BF16))

    def attend_prompt(q, kv):
        _, _, kb, vt, kmean = kv
        kmean = kmean.reshape(kmean.shape[0], -1, kmean.shape[-1])
        return _moba_prompt(q, kb, vt, kmean, n_heads, hd, g_heads=PROMPT_HEADS_PER_STEP)

    expand = _expand_matrix(past_len // MOBA_BLOCK, n_heads, V7X_LANES)

    def attend_sample(q, kv):
        return _moba_sample(q, kv[0], kv[1], cache_k, cache_v, page_table, expand)

    hist0 = jnp.zeros((n_a, n_b_p, n_hist, state_pool.shape[3]), F32)
    y_p, k_p, v_p, pool_p = _trunk(x_prompt, mods_p, mods_kv_p, hist0, 0, wts, n_heads, hd,
                                   bb=1, tt=PROMPT_POOL_TOKENS_PER_STEP, tt_proj=PROMPT_PROJ_TOKENS_PER_STEP,
                                   attend=attend_prompt, is_prompt=True)
    y_s, k_s, v_s, pool_s = _trunk(x_sample, mods_s, mods_kv_s, state_pool, past_len, wts, n_heads, hd,
                                   bb=SAMPLE_SEQS_PER_STEP, tt=x_sample.shape[1], tt_proj=x_sample.shape[1],
                                   attend=attend_sample, is_prompt=False)
    return (y_p, y_s, k_p, v_p, k_s, v_s, pool_p, pool_s)
```

```python
import functools
import math

import jax
import jax.numpy as jnp
from jax import lax
from jax.experimental import pallas as pl
from jax.experimental.pallas import tpu as pltpu

F32 = jnp.float32
BF16 = jnp.bfloat16
NEG_INF = float("-inf")
LOG2E = math.log2(math.e)

POOL_WINDOWS = (2, 4, 8, 16)
MOBA_BLOCK = 256
MOBA_TOPK = 3
ROPE_THETA = 10000.0
NORM_EPS = 1e-6

V7X_LANES = 128
V7X_SUBLANES = 8
V7X_VMEM_LIMIT_BYTES = 56 * 1024 * 1024

HIST_PAD = 16
SUM_ROWS = 16
PROMPT_HEADS_PER_STEP = 8
PROMPT_BLOCKS_PER_ITER = 8
SAMPLE_PAGES_PER_STEP = 16
SAMPLE_RING_SLOTS = 3
PROMPT_POOL_TOKENS_PER_STEP = 512
PROMPT_PROJ_TOKENS_PER_STEP = 1024
SAMPLE_SEQS_PER_STEP = 64


def _params(n_axes, n_inputs=None, fused_inputs=()):
    fusion = None if n_inputs is None else [k in fused_inputs for k in range(n_inputs)]
    return pltpu.CompilerParams(
        dimension_semantics=("arbitrary",) * n_axes, vmem_limit_bytes=V7X_VMEM_LIMIT_BYTES,
        allow_input_fusion=fusion)


def _silu(x):
    return x * (1.0 / (1.0 + jnp.exp(-x)))


def _rms(x, g):
    return x * lax.rsqrt(jnp.mean(x * x, axis=-1, keepdims=True) + NORM_EPS) * g


def _dot(a, b):
    return jnp.dot(a, b, preferred_element_type=F32)


def _bucket(idx, size, n):
    out = jnp.zeros_like(idx)
    for h in range(1, n):
        out = out + jnp.where(idx >= h * size, 1, 0)
    return out


def _dot_nt(a, b):
    return lax.dot_general(a, b, (((1,), (1,)), ((), ())), preferred_element_type=F32)


def _rope_table_kernel(cos_ref, sin_ref, *, pos0):
    t, hd = cos_ref.shape
    half = hd // 2
    lane = lax.broadcasted_iota(jnp.int32, (t, hd), 1)
    row = lax.broadcasted_iota(jnp.int32, (t, hd), 0)
    j = jnp.where(lane < half, lane, lane - half).astype(F32)
    inv = jnp.exp(j * (-2.0 * math.log(ROPE_THETA) / hd))
    ang = (row + pos0).astype(F32) * inv
    cos_ref[...] = jnp.cos(ang)
    s = jnp.sin(ang)
    sin_ref[...] = jnp.where(lane < half, -s, s)


def _rope_tables(t, hd, pos0):
    return pl.pallas_call(
        functools.partial(_rope_table_kernel, pos0=pos0),
        out_shape=(jax.ShapeDtypeStruct((t, hd), F32), jax.ShapeDtypeStruct((t, hd), F32)),
        name="rope_tables",
    )()


def _head_rows(ref, h, n_tok, n_heads):
    return (slice(None),) * (len(ref.shape) - 2) + (pl.ds(h, n_tok, stride=n_heads), slice(None))


def _rope_head(a, h, cos, sin, bb, tt, hd):
    slab = a[:, h * hd:(h + 1) * hd]
    rot = pltpu.roll(slab, hd // 2, 1)
    return slab.reshape(bb, tt, hd) * cos[None] + rot.reshape(bb, tt, hd) * sin[None]


def _ada_kernel(c_ref, w_ref, b_ref, o_ref):
    h = _silu(c_ref[...]).astype(BF16)
    o_ref[...] = _dot(h, w_ref[...].astype(BF16)) + b_ref[...]


def _ada_mods(c, w, b):
    n_l, d, n = w.shape
    r = c.shape[0]
    bn = 1024
    return pl.pallas_call(
        _ada_kernel,
        grid=(n_l, n // bn),
        in_specs=[
            pl.BlockSpec((r, d), lambda l, j: (0, 0)),
            pl.BlockSpec((None, d, bn), lambda l, j: (l, 0, j)),
            pl.BlockSpec((None, 1, bn), lambda l, j: (l, 0, j)),
        ],
        out_specs=pl.BlockSpec((None, r, bn), lambda l, j: (l, 0, j)),
        out_shape=jax.ShapeDtypeStruct((n_l, r, n), F32),
        compiler_params=_params(2),
        name="ada_mods",
    )(c, w, b.reshape(n_l, 1, n))


def _pool_kernel(x_ref, mod_ref, hist_ref, gpre_ref, gpost_ref, win_ref, wgrp_ref, pscale_ref, wout_ref,
                 y_ref, hist_out_ref, z_scr, *, pos0):
    t = pl.program_id(1)
    n_t = pl.num_programs(1)
    bb, tt, d = x_ref.shape
    w = z_scr.shape[-1]
    n_hist = hist_ref.shape[1]
    m = bb * tt
    grp = w // len(POOL_WINDOWS)

    @pl.when(t == 0)
    def _():
        z_scr[:, HIST_PAD - n_hist:HIST_PAD, :] = hist_ref[...]

    @pl.when(t > 0)
    def _():
        z_scr[:, 0:HIST_PAD, :] = z_scr[:, tt:tt + HIST_PAD, :]

    mod = mod_ref[...]
    shift, scale, gate = mod[:, :, 0:d], mod[:, :, d:2 * d], mod[:, :, 2 * d:3 * d]
    x = x_ref[...]
    h = _rms(x, gpre_ref[...]) * (1.0 + scale) + shift
    ug = _dot(h.reshape(m, d).astype(BF16), win_ref[...])
    u = ug[:, 0:w]
    z_scr[:, HIST_PAD:HIST_PAD + tt, :] = u.reshape(bb, tt, w)

    pos = pos0 + t * tt + lax.broadcasted_iota(jnp.int32, (1, tt, 1), 1)
    ys = []
    for g, win in enumerate(POOL_WINDOWS):
        sl = slice(g * grp, (g + 1) * grp)
        acc = z_scr[:, HIST_PAD:HIST_PAD + tt, sl]
        for k in range(1, win):
            acc = acc + z_scr[:, HIST_PAD - k:HIST_PAD - k + tt, sl]
        inv_cnt = 1.0 / jnp.minimum(pos + 1, win).astype(F32)
        dm = acc * inv_cnt - z_scr[:, HIST_PAD:HIST_PAD + tt, sl]
        ys.append(_dot(dm.reshape(m, grp).astype(BF16), wgrp_ref[g]))
    y = jnp.concatenate(ys, axis=-1) * pscale_ref[...]
    a = (y * _silu(ug[:, w:2 * w])).astype(BF16)
    o = _dot(a, wout_ref[...]).reshape(bb, tt, d)
    y_ref[...] = x + gate * _rms(o, gpost_ref[...])

    @pl.when(t == n_t - 1)
    def _():
        hist_out_ref[...] = z_scr[:, HIST_PAD + tt - n_hist:HIST_PAD + tt, :]


def _pool_layer(x, mod, hist, g_pre, g_post, w_in, w_grp, pool_scale, w_out, pos0, bb, tt):
    b, t, d = x.shape
    w = w_out.shape[0]
    n_hist = hist.shape[1]
    n_g, grp, _ = w_grp.shape
    const2 = lambda i, j: (0, 0)
    return pl.pallas_call(
        functools.partial(_pool_kernel, pos0=pos0),
        grid=(b // bb, t // tt),
        in_specs=[
            pl.BlockSpec((bb, tt, d), lambda i, j: (i, j, 0)),
            pl.BlockSpec((bb, 1, 3 * d), lambda i, j: (i, 0, 0)),
            pl.BlockSpec((bb, n_hist, w), lambda i, j: (i, 0, 0)),
            pl.BlockSpec((1, d), const2),
            pl.BlockSpec((1, d), const2),
            pl.BlockSpec((d, 2 * w), const2),
            pl.BlockSpec((n_g, grp, grp), lambda i, j: (0, 0, 0)),
            pl.BlockSpec((1, w), const2),
            pl.BlockSpec((w, d), const2),
        ],
        out_specs=(
            pl.BlockSpec((bb, tt, d), lambda i, j: (i, j, 0)),
            pl.BlockSpec((bb, n_hist, w), lambda i, j: (i, 0, 0)),
        ),
        out_shape=(jax.ShapeDtypeStruct((b, t, d), F32), jax.ShapeDtypeStruct((b, n_hist, w), F32)),
        scratch_shapes=[pltpu.VMEM((bb, HIST_PAD + tt, w), F32)],
        compiler_params=_params(2, 9, (5, 6, 8)),
        name="pool_layer",
    )(x, mod, hist, g_pre.reshape(1, d), g_post.reshape(1, d), w_in, w_grp, pool_scale.reshape(1, w), w_out)


def _normed(x_ref, mod_ref, g_ref):
    bb, tt, d = x_ref.shape
    mod = mod_ref[...]
    shift, scale = mod[:, :, 0:d], mod[:, :, d:2 * d]
    h = _rms(x_ref[...], g_ref[...]) * (1.0 + scale) + shift
    return h.reshape(bb * tt, d).astype(BF16)


def _kv_kernel(x_ref, mod_ref, g_ref, w_ref, cos_ref, sin_ref, k_ref, v_ref, *extra, n_heads, hd):
    bb, tt, _ = x_ref.shape
    kvw = n_heads * hd
    kv = _dot(_normed(x_ref, mod_ref, g_ref), w_ref[...])
    cos, sin = cos_ref[...], sin_ref[...]
    n_blk = tt // MOBA_BLOCK
    for h in range(n_heads):
        k_h = _rope_head(kv, h, cos, sin, bb, tt, hd)
        v_h = kv[:, kvw + h * hd:kvw + (h + 1) * hd]
        k_ref[_head_rows(k_ref, h, tt, n_heads)] = k_h
        v_ref[_head_rows(v_ref, h, tt, n_heads)] = v_h.reshape(bb, tt, hd)
        if extra:
            kb_ref, vt_ref, kmean_ref = extra
            kb_ref[:, :, h * hd:(h + 1) * hd] = k_h.astype(BF16)
            kmean_ref[:, h * hd:(h + 1) * hd] = jnp.mean(k_h.reshape(n_blk, MOBA_BLOCK, hd), axis=1)
            for j in range(n_blk):
                vt_ref[h, j] = v_h[j * MOBA_BLOCK:(j + 1) * MOBA_BLOCK, :].T.astype(BF16)


def _kv_proj(x, mod, g, w, cos, sin, n_heads, hd, bb, tt, with_attn_layouts):
    b, t, d = x.shape
    kvw = n_heads * hd
    const2 = lambda i, j: (0, 0)
    tile = lambda i, j: (i, j, 0)
    out_specs = [pl.BlockSpec((bb, tt * n_heads, hd), tile), pl.BlockSpec((bb, tt * n_heads, hd), tile)]
    out_shape = [jax.ShapeDtypeStruct((b, t * n_heads, hd), F32), jax.ShapeDtypeStruct((b, t * n_heads, hd), F32)]
    if with_attn_layouts:
        assert bb == 1 and tt % MOBA_BLOCK == 0
        n_blk = tt // MOBA_BLOCK
        out_specs += [
            pl.BlockSpec((bb, tt, kvw), tile),
            pl.BlockSpec((None, n_heads, n_blk, hd, MOBA_BLOCK), lambda i, j: (i, 0, j, 0, 0)),
            pl.BlockSpec((None, None, n_blk, kvw), lambda i, j: (i, j, 0, 0)),
        ]
        out_shape += [
            jax.ShapeDtypeStruct((b, t, kvw), BF16),
            jax.ShapeDtypeStruct((b, n_heads, t // MOBA_BLOCK, hd, MOBA_BLOCK), BF16),
            jax.ShapeDtypeStruct((b, t // tt, n_blk, kvw), F32),
        ]
    return pl.pallas_call(
        functools.partial(_kv_kernel, n_heads=n_heads, hd=hd),
        grid=(b // bb, t // tt),
        in_specs=[
            pl.BlockSpec((bb, tt, d), tile),
            pl.BlockSpec((bb, 1, mod.shape[-1]), lambda i, j: (i, 0, 0)),
            pl.BlockSpec((1, d), const2),
            pl.BlockSpec((d, 2 * kvw), const2),
            pl.BlockSpec((tt, hd), lambda i, j: (j, 0)),
            pl.BlockSpec((tt, hd), lambda i, j: (j, 0)),
        ],
        out_specs=tuple(out_specs),
        out_shape=tuple(out_shape),
        compiler_params=_params(2, 6, (3,)),
        name="kv_proj",
    )(x, mod, g.reshape(1, d), w, cos, sin)


def _q_kernel(x_ref, mod_ref, g_ref, w_ref, cos_ref, sin_ref, q_ref, gate_ref, *, n_heads, hd, head_split):
    bb, tt, _ = x_ref.shape
    aw = n_heads * hd
    qg = _dot(_normed(x_ref, mod_ref, g_ref), w_ref[...])
    cos, sin = cos_ref[...], sin_ref[...]
    for h in range(n_heads):
        q_h = _rope_head(qg, h, cos, sin, bb, tt, hd)
        if head_split:
            q_ref[_head_rows(q_ref, h, tt, n_heads)] = q_h
        else:
            q_ref[:, :, h * hd:(h + 1) * hd] = q_h.astype(BF16)
    gate_ref[...] = qg[:, aw:2 * aw].reshape(bb, tt, aw).astype(gate_ref.dtype)


def _q_proj(x, mod, g, w, cos, sin, n_heads, hd, bb, tt, head_split):
    b, t, d = x.shape
    aw = n_heads * hd
    const2 = lambda i, j: (0, 0)
    tile = lambda i, j: (i, j, 0)
    if head_split:
        q_spec = pl.BlockSpec((bb, tt * n_heads, hd), tile)
        q_shape = jax.ShapeDtypeStruct((b, t * n_heads, hd), F32)
    else:
        q_spec = pl.BlockSpec((bb, tt, aw), tile)
        q_shape = jax.ShapeDtypeStruct((b, t, aw), BF16)
    return pl.pallas_call(
        functools.partial(_q_kernel, n_heads=n_heads, hd=hd, head_split=head_split),
        grid=(b // bb, t // tt),
        in_specs=[
            pl.BlockSpec((bb, tt, d), tile),
            pl.BlockSpec((bb, 1, mod.shape[-1]), lambda i, j: (i, 0, 0)),
            pl.BlockSpec((1, d), const2),
            pl.BlockSpec((d, 2 * aw), const2),
            pl.BlockSpec((tt, hd), lambda i, j: (j, 0)),
            pl.BlockSpec((tt, hd), lambda i, j: (j, 0)),
        ],
        out_specs=(q_spec, pl.BlockSpec((bb, tt, aw), tile)),
        out_shape=(q_shape, jax.ShapeDtypeStruct((b, t, aw), q_shape.dtype)),
        compiler_params=_params(2, 6, (3,)),
        name="q_proj",
    )(x, mod, g.reshape(1, d), w, cos, sin)


def _out_kernel(o_ref, gate_ref, x_ref, mod_ref, gpost_ref, w_ref, y_ref, *, head_split):
    bb, tt, d = x_ref.shape
    aw = gate_ref.shape[-1]
    sg = _silu(gate_ref[...].astype(F32))
    if head_split:
        hd = o_ref.shape[2]
        n_heads = aw // hd
        r = jnp.zeros((bb * tt, d), F32)
        for h in range(n_heads):
            o_h = o_ref[_head_rows(o_ref, h, tt, n_heads)]
            a_h = (o_h * sg[:, :, h * hd:(h + 1) * hd]).reshape(bb * tt, hd).astype(BF16)
            r = r + _dot(a_h, w_ref[h * hd:(h + 1) * hd, :])
    else:
        a = (o_ref[...].astype(F32) * sg).reshape(bb * tt, aw).astype(BF16)
        r = _dot(a, w_ref[...])
    res_gate = mod_ref[...][:, :, 2 * d:3 * d]
    y_ref[...] = x_ref[...] + res_gate * _rms(r.reshape(bb, tt, d), gpost_ref[...])


def _out_proj(o, gate, x, mod, g_post, w, bb, tt, head_split):
    b, t, d = x.shape
    aw = gate.shape[-1]
    tile = lambda i, j: (i, j, 0)
    if head_split:
        o_spec = pl.BlockSpec((bb, o.shape[1] // t * tt, o.shape[2]), tile)
    else:
        o_spec = pl.BlockSpec((bb, tt, aw), tile)
    return pl.pallas_call(
        functools.partial(_out_kernel, head_split=head_split),
        grid=(b // bb, t // tt),
        in_specs=[
            o_spec,
            pl.BlockSpec((bb, tt, aw), tile),
            pl.BlockSpec((bb, tt, d), tile),
            pl.BlockSpec((bb, 1, mod.shape[-1]), lambda i, j: (i, 0, 0)),
            pl.BlockSpec((1, d), lambda i, j: (0, 0)),
            pl.BlockSpec((aw, d), lambda i, j: (0, 0)),
        ],
        out_specs=pl.BlockSpec((bb, tt, d), tile),
        out_shape=jax.ShapeDtypeStruct((b, t, d), F32),
        compiler_params=_params(2, 6, (5,)),
        name="out_proj",
    )(o, gate, x, mod, g_post.reshape(1, d), w)


def _select_blocks_t(gate_t, n_past):
    nb = gate_t.shape[0]
    blk = lax.broadcasted_iota(jnp.int32, gate_t.shape, 0)
    g = jnp.where(blk < n_past, gate_t, NEG_INF)
    rank = jnp.zeros(gate_t.shape, jnp.int32)
    for m in range(nb):
        gm = g[m:m + 1, :]
        rank = rank + jnp.where(gm > g, 1, jnp.where(gm == g, jnp.where(m < blk, 1, 0), 0))
    return jnp.where(blk < n_past, jnp.where(rank < MOBA_TOPK, 1, 0), 0)


def _moba_prompt_kernel(q_ref, k_ref, vt_ref, kmean_ref, o_ref, shift_scr, s_scr, *, scale, hd):
    i = pl.program_id(2)
    g_heads = q_ref.shape[1] // hd
    c = scale * LOG2E
    ones_rows = jnp.ones((SUM_ROWS, MOBA_BLOCK), BF16)

    def values_and_sums(g, n, p):
        return _dot(jnp.concatenate([vt_ref[g, n], ones_rows], axis=0), p.astype(BF16))

    start = pl.multiple_of(i * MOBA_BLOCK, MOBA_BLOCK)
    key_i = lax.broadcasted_iota(jnp.int32, (MOBA_BLOCK, MOBA_BLOCK), 0)
    qry_i = lax.broadcasted_iota(jnp.int32, (MOBA_BLOCK, MOBA_BLOCK), 1)
    causal = key_i <= qry_i

    q_ts, s_owns, m0s = [], [], []
    for g in range(g_heads):
        ln = slice(g * hd, (g + 1) * hd)
        q_t = q_ref[:, ln].astype(F32).T.astype(BF16)
        gate_t = _dot(kmean_ref[:, ln].astype(BF16), q_t)
        keep = _select_blocks_t(gate_t, i)
        shift_scr[g] = jnp.where(keep > 0, 0.0, NEG_INF)
        s = jnp.where(causal, _dot(k_ref[pl.ds(start, MOBA_BLOCK), ln], q_t) * c, NEG_INF)
        q_ts.append(q_t)
        s_owns.append(s)
        m0s.append(jnp.max(s, axis=0, keepdims=True))

    per_iter = PROMPT_BLOCKS_PER_ITER
    n_full = i // per_iter
    n_rem = i - n_full * per_iter

    def over_past_blocks(blocks_fn, carry):
        carry = lax.fori_loop(0, n_full, lambda t, cr: blocks_fn(t * per_iter, per_iter, cr), carry)
        done = n_full * per_iter
        size = per_iter // 2
        while size >= 1:
            take = (n_rem & size) != 0
            carry = lax.cond(take, lambda cr, done=done, size=size: blocks_fn(done, size, cr), lambda cr: cr, carry)
            done = done + jnp.where(take, size, 0)
            size //= 2
        return carry

    def scores_blocks(heads, n0, count, ms):
        out = list(ms)
        for u in range(count):
            n = n0 + u
            st = pl.multiple_of(n * MOBA_BLOCK, MOBA_BLOCK)
            for k, g in enumerate(heads):
                sn = _dot(k_ref[pl.ds(st, MOBA_BLOCK), g * hd:(g + 1) * hd], q_ts[g]) * c
                s_scr[g, n] = sn
                out[k] = jnp.maximum(out[k], jnp.max(sn, axis=0, keepdims=True) + shift_scr[g, pl.ds(n, 1), :])
        return tuple(out)

    def values_blocks(heads, ms, n0, count, accs):
        out = list(accs)
        for u in range(count):
            n = n0 + u
            for k, g in enumerate(heads):
                pn = jnp.exp2(s_scr[g, n] - (ms[k] - shift_scr[g, pl.ds(n, 1), :]))
                out[k] = out[k] + values_and_sums(g, n, pn)
        return tuple(out)

    def own_values(heads, ms):
        return tuple(values_and_sums(g, i, jnp.exp2(s_owns[g] - ms[k])) for k, g in enumerate(heads))

    heads = tuple(range(g_heads))
    ms = over_past_blocks(lambda n0, cnt, cr: scores_blocks(heads, n0, cnt, cr), tuple(m0s))
    accs = over_past_blocks(lambda n0, cnt, cr: values_blocks(heads, ms, n0, cnt, cr), own_values(heads, ms))
    for g in heads:
        o_ref[:, g * hd:(g + 1) * hd] = (accs[g][0:hd, :] / accs[g][hd:hd + 1, :]).T.astype(o_ref.dtype)


def _moba_prompt(q, kb, vt, kmean, n_heads, hd, g_heads):
    b, s, aw = q.shape
    n_blk = s // MOBA_BLOCK
    gw = g_heads * hd
    return pl.pallas_call(
        functools.partial(_moba_prompt_kernel, scale=hd ** -0.5, hd=hd),
        grid=(b, n_heads // g_heads, n_blk),
        in_specs=[
            pl.BlockSpec((None, MOBA_BLOCK, gw), lambda bi, h, i: (bi, i, h)),
            pl.BlockSpec((None, s, gw), lambda bi, h, i: (bi, 0, h), pipeline_mode=pl.Buffered(1)),
            pl.BlockSpec((None, g_heads, n_blk, hd, MOBA_BLOCK), lambda bi, h, i: (bi, h, 0, 0, 0),
                         pipeline_mode=pl.Buffered(1)),
            pl.BlockSpec((None, n_blk, gw), lambda bi, h, i: (bi, 0, h)),
        ],
        out_specs=pl.BlockSpec((None, MOBA_BLOCK, gw), lambda bi, h, i: (bi, i, h)),
        out_shape=jax.ShapeDtypeStruct((b, s, aw), BF16),
        scratch_shapes=[
            pltpu.VMEM((g_heads, n_blk, MOBA_BLOCK), F32),
            pltpu.VMEM((g_heads, n_blk, MOBA_BLOCK, MOBA_BLOCK), F32),
        ],
        compiler_params=_params(3),
        name="moba_prompt",
    )(q, kb, vt, kmean)


def _moba_sample_kernel(pt_ref, q_ref, kn_ref, vn_ref, expand_ref, ck_ref, cv_ref, o_ref,
                        kmean_scr, s_scr, bmax_scr, shift_scr, l_scr, acc_scr, ring, sems,
                        *, scale, n_pages, n_heads):
    seq = pl.program_id(0)
    j = pl.program_id(1)
    n_slots, pps, pr, hd = ring.shape
    ks = n_pages // pps
    steps = 2 * ks
    look = n_slots - 1
    rows = q_ref.shape[0]
    t_new = rows // n_heads
    page = pr // n_heads

    def page_copy(cache_ref, page_id, slot, a):
        return pltpu.make_async_copy(cache_ref.at[page_id], ring.at[slot, a], sems.at[slot])

    def start_step(seq_i, j_i, slot):
        is_k = j_i < ks
        first = jnp.where(is_k, j_i, j_i - ks) * pps
        for a in range(pps):
            page_id = pt_ref[seq_i, first + a]

            @pl.when(is_k)
            def _():
                page_copy(ck_ref, page_id, slot, a).start()

            @pl.when(jnp.logical_not(is_k))
            def _():
                page_copy(cv_ref, page_id, slot, a).start()

    @pl.when(jnp.logical_and(seq == 0, j == 0))
    def _():
        for ahead in range(look):
            start_step(0, ahead, ahead)

    g = seq * steps + j
    j_next = j + look
    wrap = j_next >= steps
    seq_next = jnp.where(wrap, seq + 1, seq)

    @pl.when(seq_next < pl.num_programs(0))
    def _():
        start_step(seq_next, jnp.where(wrap, j_next - steps, j_next), lax.rem(g + look, n_slots))

    slot = lax.rem(g, n_slots)
    for a in range(pps):
        page_copy(ck_ref, 0, slot, a).wait()
    ppb = MOBA_BLOCK // page
    n_blocks = n_pages // ppb
    bps = pps // ppb
    lanes = kmean_scr.shape[0]
    c = scale * LOG2E
    q_all = q_ref[...].astype(BF16)

    r_i = lax.broadcasted_iota(jnp.int32, (rows, 1), 0)
    c_i = lax.broadcasted_iota(jnp.int32, (1, lanes), 1)
    r_tok = _bucket(r_i, n_heads, t_new)
    c_grp = _bucket(c_i, n_heads, lanes // n_heads)
    same_head = (r_i - n_heads * r_tok) == (c_i - n_heads * c_grp)

    def page_wide(tile):
        return jnp.concatenate([tile] * (pr // lanes), axis=1)

    @pl.when(j == 0)
    def _():
        kmean_scr[...] = jnp.zeros_like(kmean_scr)

    @pl.when(j < ks)
    def _():
        head_bias = page_wide(jnp.where(same_head, 0.0, NEG_INF))
        sums, maxes = [], []
        for a in range(pps):
            kp = ring[slot, a]
            sums.append(jnp.sum(kp.reshape(page, n_heads, hd), axis=0))
            s = _dot_nt(q_all, kp.astype(BF16))
            s_scr[j * pps + a] = s
            maxes.append(jnp.max(s + head_bias, axis=-1, keepdims=True))
        for b in range(bps):
            blk_sum, blk_max = sums[b * ppb], maxes[b * ppb]
            for e in range(1, ppb):
                blk_sum = blk_sum + sums[b * ppb + e]
                blk_max = jnp.maximum(blk_max, maxes[b * ppb + e])
            r0 = pl.multiple_of((j * bps + b) * n_heads, n_heads)
            kmean_scr[pl.ds(r0, n_heads), :] = blk_sum * (1.0 / MOBA_BLOCK)
            bmax_scr[j * bps + b] = jnp.broadcast_to(blk_max, (rows, lanes))

    @pl.when(j == ks - 1)
    def _():
        q_pad = jnp.concatenate([q_all, jnp.zeros((lanes - rows, hd), BF16)], axis=0)
        gate_t = _dot_nt(kmean_scr[...].astype(BF16), q_pad)
        tiles = [gate_t[n * n_heads:(n + 1) * n_heads, :] for n in range(n_blocks)]
        sub_i = lax.broadcasted_iota(jnp.int32, (n_heads, lanes), 0)
        own = jnp.where(c_i < rows, jnp.where(sub_i == c_i - n_heads * c_grp, 1.0, 0.0), 0.0)
        keep_tiles = []
        for n in range(n_blocks):
            rank = jnp.zeros((n_heads, lanes), jnp.int32)
            for m in range(n_blocks):
                if m < n:
                    rank = rank + jnp.where(tiles[m] >= tiles[n], 1, 0)
                elif m > n:
                    rank = rank + jnp.where(tiles[m] > tiles[n], 1, 0)
            keep_tiles.append(jnp.where(rank < MOBA_TOPK, own, 0.0))
        keep_tiles.append(jnp.zeros((lanes - n_blocks * n_heads, lanes), F32))
        keep = jnp.concatenate(keep_tiles, axis=0).T
        keep_x = _dot(keep.astype(BF16), expand_ref[...])[0:rows, :]

        pad = jnp.zeros((lanes - rows, hd), F32)
        kn = jnp.concatenate([kn_ref[...], pad], axis=0).astype(BF16)
        vn = jnp.concatenate([vn_ref[...], pad], axis=0).astype(BF16)
        own_ok = jnp.where(c_i < rows, jnp.where(same_head, jnp.where(c_grp <= r_tok, 1, 0), 0), 0)
        s_own = jnp.where(own_ok > 0, _dot_nt(q_all, kn), NEG_INF)

        kept = [keep_x[:, n * lanes:(n + 1) * lanes] > 0.5 for n in range(n_blocks)]
        m_lane = jnp.full((rows, lanes), NEG_INF, F32)
        for n in range(n_blocks):
            m_lane = jnp.maximum(m_lane, jnp.where(kept[n], bmax_scr[n], NEG_INF))
        m_fin = jnp.maximum(jnp.max(s_own, axis=-1, keepdims=True),
                            jnp.max(m_lane, axis=-1, keepdims=True))
        neg_m = jnp.broadcast_to(-m_fin, (rows, lanes))
        for n in range(n_blocks):
            shift_scr[n] = jnp.where(kept[n], neg_m, NEG_INF)
        p_own = jnp.exp2((s_own - m_fin) * c)
        l_scr[...] = jnp.broadcast_to(jnp.sum(p_own, axis=-1, keepdims=True), l_scr.shape)
        acc_scr[...] = _dot(p_own.astype(BF16), vn)

    @pl.when(j >= ks)
    def _():
        acc = acc_scr[...]
        l_add = jnp.zeros((rows, 1), F32)
        for b in range(bps):
            shift = page_wide(shift_scr[(j - ks) * bps + b])
            for e in range(ppb):
                a = b * ppb + e
                pn = jnp.exp2((s_scr[(j - ks) * pps + a] + shift) * c)
                l_add = l_add + jnp.sum(pn, axis=-1, keepdims=True)
                acc = acc + _dot(pn.astype(BF16), ring[slot, a].astype(BF16))
        acc_scr[...] = acc
        l_scr[...] = l_scr[...] + l_add

    @pl.when(j == 2 * ks - 1)
    def _():
        o_ref[...] = acc_scr[...] / l_scr[...]


def _moba_sample(q, k_new, v_new, cache_k, cache_v, page_table, expand):
    n_phys, page, n_heads, hd = cache_k.shape
    db, t_new = q.shape[0], q.shape[1] // n_heads
    cache_k = cache_k.reshape(n_phys, page * n_heads, hd)
    cache_v = cache_v.reshape(n_phys, page * n_heads, hd)
    n_pages = page_table.shape[1]
    pps = SAMPLE_PAGES_PER_STEP
    n_slots = SAMPLE_RING_SLOTS
    assert MOBA_BLOCK % page == 0 and pps % (MOBA_BLOCK // page) == 0 and n_pages % pps == 0
    ks = n_pages // pps
    assert n_slots - 1 <= 2 * ks
    rows = t_new * n_heads
    lanes = V7X_LANES
    assert rows <= lanes and (n_pages * page // MOBA_BLOCK) * n_heads <= lanes and lanes % n_heads == 0
    assert (page * n_heads) % lanes == 0

    seq = lambda s, j, pt: (s, 0, 0)
    new_spec = pl.BlockSpec((None, rows, hd), seq)
    grid_spec = pltpu.PrefetchScalarGridSpec(
        num_scalar_prefetch=1,
        grid=(db, 2 * ks),
        in_specs=[new_spec, new_spec, new_spec, pl.BlockSpec(expand.shape, lambda s, j, pt: (0, 0)),
                  pl.BlockSpec(memory_space=pl.ANY), pl.BlockSpec(memory_space=pl.ANY)],
        out_specs=new_spec,
        scratch_shapes=[
            pltpu.VMEM((lanes, hd), F32),
            pltpu.VMEM((n_pages, rows, page * n_heads), F32),
            pltpu.VMEM((n_pages * page // MOBA_BLOCK, rows, lanes), F32),
            pltpu.VMEM((n_pages * page // MOBA_BLOCK, rows, lanes), F32),
            pltpu.VMEM((rows, hd), F32),
            pltpu.VMEM((rows, hd), F32),
            pltpu.VMEM((n_slots, pps, page * n_heads, hd), F32),
            pltpu.SemaphoreType.DMA((n_slots,)),
        ],
    )
    return pl.pallas_call(
        functools.partial(_moba_sample_kernel, scale=hd ** -0.5, n_pages=n_pages, n_heads=n_heads),
        grid_spec=grid_spec,
        out_shape=jax.ShapeDtypeStruct((db, rows, hd), F32),
        compiler_params=_params(2),
        name="moba_sample",
    )(page_table, q, k_new, v_new, expand, cache_k, cache_v)


def _expand_matrix(n_blocks, n_heads, lanes):
    r = jnp.arange(lanes)[:, None]
    c = jnp.arange(n_blocks * lanes)[None, :]
    return ((r // n_heads == c // lanes) & (r % n_heads == c % n_heads) & (r < n_blocks * n_heads)).astype(BF16)


def _trunk(x, mods, mods_kv, hist, pos0, wts, n_heads, hd, bb, tt, tt_proj, attend, is_prompt):
    t = x.shape[1]
    n_a = wts["w_in_pool"].shape[0]
    n_b = wts["w_in_attn"].shape[0]
    cos, sin = _rope_tables(t, hd, pos0)
    new_hist = []
    for l in range(n_a):
        x, hn = _pool_layer(x, mods[l], hist[l], wts["g_pre"][l], wts["g_post"][l], wts["w_in_pool"][l],
                            wts["w_pool_group"][l], wts["pool_scale"][l], wts["w_out_pool"][l], pos0, bb, tt)
        new_hist.append(hn)
    kv = _kv_proj(x, mods_kv, wts["g_kv"], wts["w_kv"], cos, sin, n_heads, hd, bb, tt_proj, is_prompt)
    for jl in range(n_b):
        l = n_a + jl
        q, gate = _q_proj(x, mods[l], wts["g_pre"][l], wts["w_in_attn"][jl], cos, sin, n_heads, hd, bb, tt_proj,
                          head_split=not is_prompt)
        o = attend(q, kv)
        x = _out_proj(o, gate, x, mods[l], wts["g_post"][l], wts["w_out_attn"][jl], bb, tt_proj,
                      head_split=not is_prompt)
    b = x.shape[0]
    return x, kv[0].reshape(b, t, n_heads, hd), kv[1].reshape(b, t, n_heads, hd), jnp.stack(new_hist)


def kernel(x_prompt, x_sample, cache_k, cache_v, state_pool, page_table, c_prompt, c_sample, w_ada, b_ada,
           g_pre, g_post, w_in_pool, w_pool_group, pool_scale, w_out_pool, g_kv, w_ada_kv, b_ada_kv, w_kv,
           w_in_attn, w_out_attn):
    n_b_p = x_prompt.shape[0]
    page, n_heads, hd = cache_k.shape[1], cache_k.shape[2], cache_k.shape[3]
    past_len = page_table.shape[1] * page
    n_a = w_in_pool.shape[0]
    n_hist = state_pool.shape[2]

    c_all = jnp.concatenate([c_prompt, c_sample], axis=0)
    r = c_all.shape[0]
    r_pad = -(-r // V7X_SUBLANES) * V7X_SUBLANES
    c_all = jnp.pad(c_all, ((0, r_pad - r), (0, 0)))
    mods_all = _ada_mods(c_all, w_ada, b_ada)
    mods_kv_all = _ada_mods(c_all, w_ada_kv[None], b_ada_kv[None])[0]
    mods_p = [mods_all[l, 0:n_b_p][:, None, :] for l in range(mods_all.shape[0])]
    mods_s = [mods_all[l, n_b_p:r][:, None, :] for l in range(mods_all.shape[0])]
    mods_kv_p = mods_kv_all[0:n_b_p][:, None, :]
    mods_kv_s = mods_kv_all[n_b_p:r][:, None, :]

    wts = dict(
        g_pre=g_pre, g_post=g_post, pool_scale=pool_scale, g_kv=g_kv,
        w_in_pool=w_in_pool.astype(BF16), w_pool_group=w_pool_group.astype(BF16),
        w_out_pool=w_out_pool.astype(BF16), w_kv=w_kv.astype(BF16),
        w_in_attn=w_in_attn.astype(BF16), w_out_attn=w_out_attn.astype(BF16))

    def attend_prompt(q, kv):
        _, _, kb, vt, kmean = kv
        kmean = kmean.reshape(kmean.shape[0], -1, kmean.shape[-1])
        return _moba_prompt(q, kb, vt, kmean, n_heads, hd, g_heads=PROMPT_HEADS_PER_STEP)

    expand = _expand_matrix(past_len // MOBA_BLOCK, n_heads, V7X_LANES)

    def attend_sample(q, kv):
        return _moba_sample(q, kv[0], kv[1], cache_k, cache_v, page_table, expand)

    hist0 = jnp.zeros((n_a, n_b_p, n_hist, state_pool.shape[3]), F32)
    y_p, k_p, v_p, pool_p = _trunk(x_prompt, mods_p, mods_kv_p, hist0, 0, wts, n_heads, hd,
                                   bb=1, tt=PROMPT_POOL_TOKENS_PER_STEP, tt_proj=PROMPT_PROJ_TOKENS_PER_STEP,
                                   attend=attend_prompt, is_prompt=True)
    y_s, k_s, v_s, pool_s = _trunk(x_sample, mods_s, mods_kv_s, state_pool, past_len, wts, n_heads, hd,
                                   bb=SAMPLE_SEQS_PER_STEP, tt=x_sample.shape[1], tt_proj=x_sample.shape[1],
                                   attend=attend_sample, is_prompt=False)
    return (y_p, y_s, k_p, v_p, k_s, v_s, pool_p, pool_s)
```
